```python
import math
import jax
import jax.numpy as jnp
from jax import lax
import numpy as np

D_MODEL = 2048
BATCH = 16
SEQ = 2048
DEPTH = 1
DEC_BATCH = 32
DEC_SEQ = 4
PAST_LEN = 16384
PAGE_SIZE = 128

MIX_WIDTH = D_MODEL
ATT_WIDTH = MIX_WIDTH // 2
SSM_WIDTH = MIX_WIDTH - ATT_WIDTH
HEAD_DIM = 64
N_HEADS = ATT_WIDTH // HEAD_DIM
SSM_GROUP = 16
N_SSM_GROUPS = SSM_WIDTH // SSM_GROUP
SSM_STATE = 64
Q_BLOCK = 128
SB_BIAS_INIT = -8.0
N_EXPERTS = 64
TOP_K = 6
D_EXPERT = D_MODEL // 4
D_SHARED = D_MODEL // 4
ROUTED_SCALE = 2.5
RMS_EPS = 1e-6
N_MOD = 6
MAX_DISPATCH_BLOCK = 128
MIN_DISPATCH_BLOCK = 8

kernel_name = 'hybrid_s5_stickbreak_moe_adaln_step'


def rmsnorm(x, g):
    xf = x.astype(jnp.float32)
    inv = lax.rsqrt(jnp.mean(xf * xf, axis=-1, keepdims=True) + RMS_EPS)
    return (xf * inv * g.astype(jnp.float32)).astype(x.dtype)


def swiglu(x, w_gate, w_up, w_down):
    return jnp.dot(jax.nn.silu(jnp.dot(x, w_gate)) * jnp.dot(x, w_up), w_down)


def stick_breaking_attention(q, k, v, q_pos, k_pos, bias):
    B, Lq, H, Dh = q.shape
    qb = min(Q_BLOCK, Lq)
    nb = Lq // qb
    q_blocks = q.reshape(B, nb, qb, H, Dh).transpose(1, 0, 2, 3, 4)
    pos_blocks = q_pos.reshape(nb, qb)
    scale = Dh ** -0.5
    b_h = bias.astype(jnp.float32)[None, :, None, None]

    def one_block(args):
        qi, pi = args
        z = jnp.einsum('bqhd,bkhd->bhqk', qi, k, preferred_element_type=jnp.float32) * scale + b_h
        causal = k_pos[None, :] < pi[:, None]
        log_keep = jnp.where(causal, jax.nn.log_sigmoid(-z), 0.0)
        after = lax.cumsum(log_keep, axis=3, reverse=True) - log_keep
        a = jnp.where(causal, jnp.exp(jax.nn.log_sigmoid(z) + after), 0.0)
        return jnp.einsum('bhqk,bkhd->bqhd', a.astype(v.dtype), v, preferred_element_type=jnp.float32)

    o = lax.map(one_block, (q_blocks, pos_blocks))
    return o.transpose(1, 0, 2, 3, 4).reshape(B, Lq, H, Dh)


def _complex_affine_combine(e1, e2):
    a1r, a1i, b1r, b1i = e1
    a2r, a2i, b2r, b2i = e2
    return (a2r * a1r - a2i * a1i,
            a2r * a1i + a2i * a1r,
            a2r * b1r - a2i * b1i + b2r,
            a2r * b1i + a2i * b1r + b2i)


def s5_mixer(u, h0_re, h0_im, a_re, a_im, log_dt, b_re, b_im, c_re, c_im, d_skip):
    B, L, _ = u.shape
    f32 = jnp.float32
    uf = u.astype(f32).reshape(B, L, N_SSM_GROUPS, SSM_GROUP)
    a_re = a_re.astype(f32)
    a_im = a_im.astype(f32)
    dt = jnp.exp(log_dt.astype(f32))[:, None]
    mag = jnp.exp(dt * a_re)
    abar_re = mag * jnp.cos(dt * a_im)
    abar_im = mag * jnp.sin(dt * a_im)
    den = a_re * a_re + a_im * a_im
    nr = abar_re - 1.0
    ni = abar_im
    coef_re = (nr * a_re + ni * a_im) / den
    coef_im = (ni * a_re - nr * a_im) / den
    br = b_re.astype(f32)
    bi = b_im.astype(f32)
    bbar_re = coef_re[..., None] * br - coef_im[..., None] * bi
    bbar_im = coef_re[..., None] * bi + coef_im[..., None] * br
    bu_re = jnp.einsum('gnc,blgc->blgn', bbar_re, uf)
    bu_im = jnp.einsum('gnc,blgc->blgn', bbar_im, uf)
    shp = (1, L, N_SSM_GROUPS, SSM_STATE)
    a_seq_re = jnp.broadcast_to(abar_re[None, None], shp)
    a_seq_im = jnp.broadcast_to(abar_im[None, None], shp)
    p_re, p_im, s_re, s_im = lax.associative_scan(
        _complex_affine_combine, (a_seq_re, a_seq_im, bu_re, bu_im), axis=1)
    h0r = h0_re.astype(f32)[:, None]
    h0i = h0_im.astype(f32)[:, None]
    x_re = s_re + p_re * h0r - p_im * h0i
    x_im = s_im + p_re * h0i + p_im * h0r
    y = (jnp.einsum('gcn,blgn->blgc', c_re.astype(f32), x_re)
         - jnp.einsum('gcn,blgn->blgc', c_im.astype(f32), x_im)
         + d_skip.astype(f32).reshape(N_SSM_GROUPS, SSM_GROUP) * uf)
    return y.reshape(B, L, SSM_WIDTH), x_re[:, -1], x_im[:, -1]


def moe_ffn(h, w_router, router_bias, we_gate, we_up, we_down, ws_gate, ws_up, ws_down):
    B, L, D = h.shape
    T = B * L
    x = h.reshape(T, D)
    scores = jax.nn.sigmoid(jnp.dot(x, w_router, preferred_element_type=jnp.float32))
    _, idx = lax.top_k(scores + router_bias.astype(jnp.float32), TOP_K)
    sel = jnp.take_along_axis(scores, idx, axis=-1)
    gates = sel / jnp.sum(sel, axis=-1, keepdims=True) * ROUTED_SCALE
    n_assign = T * TOP_K
    blk = max(MIN_DISPATCH_BLOCK, min(MAX_DISPATCH_BLOCK, n_assign // N_EXPERTS))
    n_blocks = -(-n_assign // blk) + N_EXPERTS
    e_flat = idx.reshape(-1).astype(jnp.int32)
    tok_flat = jnp.arange(n_assign, dtype=jnp.int32) // TOP_K
    g_flat = gates.reshape(-1)
    order = jnp.argsort(e_flat)
    e_sorted = e_flat[order]
    counts = jnp.bincount(e_flat, length=N_EXPERTS).astype(jnp.int32)
    padded = (counts + blk - 1) // blk * blk
    pad_end = jnp.cumsum(padded)
    pad_start = pad_end - padded
    start = jnp.cumsum(counts) - counts
    dest = pad_start[e_sorted] + jnp.arange(n_assign, dtype=jnp.int32) - start[e_sorted]
    rows_tok = jnp.full((n_blocks * blk,), T, jnp.int32).at[dest].set(tok_flat[order])
    rows_gate = jnp.zeros((n_blocks * blk,), jnp.float32).at[dest].set(g_flat[order])
    block_start = jnp.arange(n_blocks, dtype=jnp.int32) * blk
    block_expert = jnp.minimum(jnp.searchsorted(pad_end, block_start, side='right'),
                               N_EXPERTS - 1).astype(jnp.int32)
    x_pad = jnp.concatenate([x, jnp.zeros((1, D), x.dtype)], axis=0)

    def run_block(args):
        tok_b, e = args
        return swiglu(x_pad[tok_b], we_gate[e], we_up[e], we_down[e])

    y_rows = lax.map(run_block, (rows_tok.reshape(n_blocks, blk), block_expert))
    y_rows = y_rows.reshape(n_blocks * blk, D) * rows_gate[:, None]
    routed = jax.ops.segment_sum(y_rows, rows_tok, num_segments=T + 1)[:T]
    shared = swiglu(x, ws_gate, ws_up, ws_down)
    return (routed + shared).astype(h.dtype).reshape(B, L, D)


def hybrid_layer(x, c, past_k, past_v, h0_re, h0_im, p):
    B, L, _ = x.shape
    mod = jnp.dot(jax.nn.silu(c), p['w_ada']) + p['b_ada']
    sh1, sc1, g1, sh2, sc2, g2 = jnp.split(mod[:, None, :], N_MOD, axis=-1)
    h = rmsnorm(x, p['norm1_g']) * (1.0 + sc1) + sh1
    proj = jnp.dot(h, p['w_in'])
    q = proj[..., :ATT_WIDTH].reshape(B, L, N_HEADS, HEAD_DIM)
    k = proj[..., ATT_WIDTH:2 * ATT_WIDTH].reshape(B, L, N_HEADS, HEAD_DIM)
    v = proj[..., 2 * ATT_WIDTH:3 * ATT_WIDTH].reshape(B, L, N_HEADS, HEAD_DIM)
    u = proj[..., 3 * ATT_WIDTH:]
    if past_k is None:
        past_len = 0
        k_all, v_all = k, v
    else:
        past_len = past_k.shape[1]
        k_all = jnp.concatenate([past_k.astype(k.dtype), k], axis=1)
        v_all = jnp.concatenate([past_v.astype(v.dtype), v], axis=1)
    q_pos = past_len + jnp.arange(L, dtype=jnp.int32)
    k_pos = jnp.arange(past_len + L, dtype=jnp.int32)
    o_att = stick_breaking_attention(q, k_all, v_all, q_pos, k_pos,
                                     p['sb_bias']).reshape(B, L, ATT_WIDTH).astype(x.dtype)
    y_ssm, h_re, h_im = s5_mixer(u, h0_re, h0_im, p['ssm_a_re'], p['ssm_a_im'], p['ssm_log_dt'],
                                 p['ssm_b_re'], p['ssm_b_im'], p['ssm_c_re'], p['ssm_c_im'], p['ssm_d'])
    y_ssm = jax.nn.gelu(y_ssm)
    o_ssm = (y_ssm * jax.nn.sigmoid(jnp.dot(y_ssm, p['w_glu']) + p['b_glu'])).astype(x.dtype)
    merged = jnp.concatenate([rmsnorm(o_ssm, p['ssm_out_g']), rmsnorm(o_att, p['attn_out_g'])], axis=-1)
    x = x + g1 * jnp.dot(merged, p['w_out'])
    h2 = rmsnorm(x, p['norm2_g']) * (1.0 + sc2) + sh2
    x = x + g2 * moe_ffn(h2, p['w_router'], p['router_bias'], p['we_gate'], p['we_up'], p['we_down'],
                         p['ws_gate'], p['ws_up'], p['ws_down'])
    return x, k, v, h_re, h_im


def setup_inputs(seed: int = 0) -> dict:
    key = jax.random.key(seed)
    ks = jax.random.split(key, 40)
    f32 = jnp.float32
    n_pages = PAST_LEN // PAGE_SIZE
    n_used = DEC_BATCH * n_pages
    n_pool = n_used + max(1, n_used // 4)
    nrm = lambda k, shp, s: jax.random.normal(k, shp, f32) * s
    G, N, HG = N_SSM_GROUPS, SSM_STATE, SSM_GROUP
    page_table = jax.random.permutation(ks[8], n_pool)[:n_used].reshape(DEC_BATCH, n_pages).astype(jnp.int32)
    a_im_init = math.pi * jnp.arange(N, dtype=f32)
    return {
        'x_prompt': nrm(ks[0], (BATCH, SEQ, D_MODEL), 1.0),
        'x_sample': nrm(ks[1], (DEC_BATCH, DEC_SEQ, D_MODEL), 1.0),
        'c_prompt': nrm(ks[2], (BATCH, D_MODEL), 1.0),
        'c_sample': nrm(ks[3], (DEC_BATCH, D_MODEL), 1.0),
        'cache_k': nrm(ks[4], (DEPTH, n_pool, PAGE_SIZE, N_HEADS, HEAD_DIM), 1.0),
        'cache_v': nrm(ks[5], (DEPTH, n_pool, PAGE_SIZE, N_HEADS, HEAD_DIM), 1.0),
        'state_ssm_re': nrm(ks[6], (DEPTH, DEC_BATCH, G, N), 0.1),
        'state_ssm_im': nrm(ks[7], (DEPTH, DEC_BATCH, G, N), 0.1),
        'page_table': page_table,
        'norm1_g': 1.0 + nrm(ks[9], (DEPTH, D_MODEL), 0.02),
        'norm2_g': 1.0 + nrm(ks[10], (DEPTH, D_MODEL), 0.02),
        'w_ada': nrm(ks[11], (DEPTH, D_MODEL, N_MOD * D_MODEL), 0.5 * D_MODEL ** -0.5),
        'b_ada': nrm(ks[12], (DEPTH, N_MOD * D_MODEL), 0.02),
        'w_in': nrm(ks[13], (DEPTH, D_MODEL, 3 * ATT_WIDTH + SSM_WIDTH), D_MODEL ** -0.5),
        'sb_bias': SB_BIAS_INIT + nrm(ks[36], (DEPTH, N_HEADS), 0.1),
        'ssm_a_re': -0.5 + nrm(ks[14], (DEPTH, G, N), 0.01),
        'ssm_a_im': a_im_init + nrm(ks[15], (DEPTH, G, N), 0.01),
        'ssm_log_dt': jax.random.uniform(ks[16], (DEPTH, G), f32, math.log(1e-3), math.log(1e-1)),
        'ssm_b_re': nrm(ks[17], (DEPTH, G, N, HG), (2 * HG) ** -0.5),
        'ssm_b_im': nrm(ks[18], (DEPTH, G, N, HG), (2 * HG) ** -0.5),
        'ssm_c_re': nrm(ks[19], (DEPTH, G, HG, N), (2 * N) ** -0.5),
        'ssm_c_im': nrm(ks[20], (DEPTH, G, HG, N), (2 * N) ** -0.5),
        'ssm_d': nrm(ks[21], (DEPTH, SSM_WIDTH), 1.0),
        'w_glu': nrm(ks[22], (DEPTH, SSM_WIDTH, SSM_WIDTH), SSM_WIDTH ** -0.5),
        'b_glu': nrm(ks[23], (DEPTH, SSM_WIDTH), 0.02),
        'ssm_out_g': 1.0 + nrm(ks[24], (DEPTH, SSM_WIDTH), 0.02),
        'attn_out_g': 1.0 + nrm(ks[25], (DEPTH, ATT_WIDTH), 0.02),
        'w_out': nrm(ks[26], (DEPTH, MIX_WIDTH, D_MODEL), MIX_WIDTH ** -0.5),
        'w_router': nrm(ks[27], (DEPTH, D_MODEL, N_EXPERTS), D_MODEL ** -0.5),
        'router_bias': nrm(ks[28], (DEPTH, N_EXPERTS), 0.01),
        'we_gate': nrm(ks[29], (DEPTH, N_EXPERTS, D_MODEL, D_EXPERT), D_MODEL ** -0.5),
        'we_up': nrm(ks[30], (DEPTH, N_EXPERTS, D_MODEL, D_EXPERT), D_MODEL ** -0.5),
        'we_down': nrm(ks[31], (DEPTH, N_EXPERTS, D_EXPERT, D_MODEL), D_EXPERT ** -0.5),
        'ws_gate': nrm(ks[32], (DEPTH, D_MODEL, D_SHARED), D_MODEL ** -0.5),
        'ws_up': nrm(ks[33], (DEPTH, D_MODEL, D_SHARED), D_MODEL ** -0.5),
        'ws_down': nrm(ks[34], (DEPTH, D_SHARED, D_MODEL), D_SHARED ** -0.5),
        'normf_g': 1.0 + nrm(ks[35], (D_MODEL,), 0.02),
    }


def reference(x_prompt, x_sample, c_prompt, c_sample, cache_k, cache_v, state_ssm_re, state_ssm_im,
              page_table, norm1_g, norm2_g, w_ada, b_ada, w_in, sb_bias, ssm_a_re, ssm_a_im, ssm_log_dt,
              ssm_b_re, ssm_b_im, ssm_c_re, ssm_c_im, ssm_d, w_glu, b_glu, ssm_out_g, attn_out_g,
              w_out, w_router, router_bias, we_gate, we_up, we_down, ws_gate, ws_up, ws_down, normf_g):
    n_pages = page_table.shape[1]
    page_size = cache_k.shape[2]
    dec_b = x_sample.shape[0]
    xp, xs = x_prompt, x_sample
    kp_l, vp_l, hrp_l, hip_l, ks_l, vs_l, hrs_l, his_l = [], [], [], [], [], [], [], []
    for l in range(DEPTH):
        p = {
            'norm1_g': norm1_g[l], 'norm2_g': norm2_g[l], 'w_ada': w_ada[l], 'b_ada': b_ada[l],
            'w_in': w_in[l], 'sb_bias': sb_bias[l], 'ssm_a_re': ssm_a_re[l], 'ssm_a_im': ssm_a_im[l],
            'ssm_log_dt': ssm_log_dt[l], 'ssm_b_re': ssm_b_re[l], 'ssm_b_im': ssm_b_im[l],
            'ssm_c_re': ssm_c_re[l], 'ssm_c_im': ssm_c_im[l],
            'ssm_d': ssm_d[l], 'w_glu': w_glu[l], 'b_glu': b_glu[l], 'ssm_out_g': ssm_out_g[l],
            'attn_out_g': attn_out_g[l], 'w_out': w_out[l], 'w_router': w_router[l],
            'router_bias': router_bias[l], 'we_gate': we_gate[l], 'we_up': we_up[l], 'we_down': we_down[l],
            'ws_gate': ws_gate[l], 'ws_up': ws_up[l], 'ws_down': ws_down[l],
        }
        zeros_state = jnp.zeros((xp.shape[0], N_SSM_GROUPS, SSM_STATE), jnp.float32)
        xp, k_p, v_p, hr_p, hi_p = hybrid_layer(xp, c_prompt, None, None, zeros_state, zeros_state, p)
        past_k = cache_k[l][page_table].reshape(dec_b, n_pages * page_size, N_HEADS, HEAD_DIM)
        past_v = cache_v[l][page_table].reshape(dec_b, n_pages * page_size, N_HEADS, HEAD_DIM)
        xs, k_s, v_s, hr_s, hi_s = hybrid_layer(xs, c_sample, past_k, past_v,
                                                state_ssm_re[l], state_ssm_im[l], p)
        kp_l.append(k_p); vp_l.append(v_p); hrp_l.append(hr_p); hip_l.append(hi_p)
        ks_l.append(k_s); vs_l.append(v_s); hrs_l.append(hr_s); his_l.append(hi_s)
    y_prompt = rmsnorm(xp, normf_g)
    y_sample = rmsnorm(xs, normf_g)
    return (y_prompt, y_sample, jnp.stack(kp_l), jnp.stack(vp_l), jnp.stack(hrp_l), jnp.stack(hip_l),
            jnp.stack(ks_l), jnp.stack(vs_l), jnp.stack(hrs_l), jnp.stack(his_l))
```

```python
import functools
import math

import jax
import jax.numpy as jnp
from jax import lax
from jax.experimental import pallas as pl
from jax.experimental.pallas import tpu as pltpu

F32 = jnp.float32
BF16 = jnp.bfloat16

RMS_EPS = 1e-6
TOP_K = 6
ROUTED_SCALE = 2.5
N_MOD = 6

LANES = 128
VMEM_LIMIT = 56 << 20
MOE_BLOCK = 256
FINAL_TILE = 128
ATT_TILE = 256
SSM_CHUNK = 512
IDX_LANES = 128


def _cparams(*sem):
    return pltpu.CompilerParams(dimension_semantics=sem, vmem_limit_bytes=VMEM_LIMIT)


def _const_spec(shape):
    nd = len(shape)
    return pl.BlockSpec(shape, lambda *_: (0,) * nd, pipeline_mode=pl.Buffered(1))


def _rms(x, g):
    inv = lax.rsqrt(jnp.mean(x * x, axis=-1, keepdims=True) + RMS_EPS)
    return x * inv * g


def _softplus(z):
    return jnp.maximum(z, 0.0) + jnp.log(1.0 + jnp.exp(-jnp.abs(z)))


def _split_bf16(x):
    hi = x.astype(BF16)
    lo = (x - hi.astype(F32)).astype(BF16)
    return hi, lo


def _dot(a, b):
    return jnp.dot(a, b, preferred_element_type=F32)


def _dot_nt(a, b):
    return lax.dot_general(a, b, (((1,), (1,)), ((), ())), preferred_element_type=F32)


def _suffix_matrix(n):
    j = lax.broadcasted_iota(jnp.int32, (n, n), 0)
    s = lax.broadcasted_iota(jnp.int32, (n, n), 1)
    return (j > s).astype(BF16)


def _mod_kernel(c_ref, w_ref, b_ref, o_ref):
    c = c_ref[...]
    a = (c * jax.nn.sigmoid(c)).astype(BF16)
    o_ref[...] = _dot(a, w_ref[...].astype(BF16)) + b_ref[...]


def _modulation(c, w_ada, b_ada):
    rows, d = c.shape
    n = w_ada.shape[1]
    tn = 1024
    return pl.pallas_call(
        _mod_kernel,
        grid=(n // tn,),
        in_specs=[_const_spec((rows, d)),
                  pl.BlockSpec((d, tn), lambda j: (0, j)),
                  pl.BlockSpec((1, tn), lambda j: (0, j))],
        out_specs=pl.BlockSpec((rows, tn), lambda j: (0, j)),
        out_shape=jax.ShapeDtypeStruct((rows, n), F32),
        compiler_params=_cparams("parallel"),
        name="mod",
    )(c, w_ada, b_ada.reshape(1, n))


def _inproj_kernel(x_ref, sh_ref, sc_ref, g_ref, w_ref, q_ref, k_ref, v_ref, kb_ref, vb_ref, u_ref, *, w_att, scale):
    x = x_ref[...]
    h = (_rms(x, g_ref[...]) * (1.0 + sc_ref[0]) + sh_ref[0]).astype(BF16)
    q = _dot(h, w_ref[:, 0:w_att])
    q_ref[...] = (q * scale).astype(BF16)
    k = _dot(h, w_ref[:, w_att:2 * w_att])
    k_ref[...] = k
    kb_ref[...] = k.astype(BF16)
    v = _dot(h, w_ref[:, 2 * w_att:3 * w_att])
    v_ref[...] = v
    vb_ref[...] = v.astype(BF16)
    u_ref[...] = _dot(h, w_ref[:, 3 * w_att:])


def _in_proj(x2, sh, sc, g, w_bf, *, w_att, head_dim, tm, tiles_per_group, time_major_batches):
    t, d = x2.shape
    n = w_bf.shape[1]
    w_ssm = n - 3 * w_att
    mrows = sh.shape[1]
    row = lambda i: (i, 0)
    grp = lambda i: (i // tiles_per_group, 0, 0)
    if time_major_batches:
        u_shape = (t // time_major_batches, time_major_batches * w_ssm)
        u_spec = pl.BlockSpec((tm, w_ssm), lambda i: (i % tiles_per_group, i // tiles_per_group))
    else:
        u_shape = (t, w_ssm)
        u_spec = pl.BlockSpec((tm, w_ssm), row)
    return pl.pallas_call(
        functools.partial(_inproj_kernel, w_att=w_att, scale=head_dim ** -0.5),
        grid=(t // tm,),
        in_specs=[pl.BlockSpec((tm, d), row),
                  pl.BlockSpec((1, mrows, d), grp),
                  pl.BlockSpec((1, mrows, d), grp),
                  _const_spec((1, d)),
                  _const_spec((d, n))],
        out_specs=[pl.BlockSpec((tm, w_att), row)] * 5 + [u_spec],
        out_shape=[jax.ShapeDtypeStruct((t, w_att), BF16),
                   jax.ShapeDtypeStruct((t, w_att), F32),
                   jax.ShapeDtypeStruct((t, w_att), F32),
                   jax.ShapeDtypeStruct((t, w_att), BF16),
                   jax.ShapeDtypeStruct((t, w_att), BF16),
                   jax.ShapeDtypeStruct(u_shape, F32)],
        compiler_params=_cparams("parallel"),
        name="in_proj",
    )(x2, sh, sc, g, w_bf)


def _sb_tile(qh, ks, vs, bias, carry, acc, umat, causal):
    z = _dot_nt(qh, ks) + bias
    sp = _softplus(z)
    if causal is not None:
        sp = jnp.where(causal, sp, 0.0)
    hi, lo = _split_bf16(sp)
    suffix = _dot(hi, umat) + _dot(lo, umat)
    a = jnp.exp(z - sp - suffix - carry)
    if causal is not None:
        a = jnp.where(causal, a, 0.0)
    acc = acc + _dot(a.astype(BF16), vs)
    carry = carry + jnp.sum(sp, axis=1, keepdims=True)
    return carry, acc


def _attn_kernel(bias_ref, q_ref, k_ref, v_ref, o_ref, *, tile, head_dim):
    hp = pl.program_id(1)
    qi = pl.program_id(2)
    q = q_ref[0]
    heads_per_block = LANES // head_dim
    lane_head = lax.broadcasted_iota(jnp.int32, (1, LANES), 1) // head_dim
    umat = _suffix_matrix(tile)
    t_idx = lax.broadcasted_iota(jnp.int32, (tile, tile), 0)
    s_idx = lax.broadcasted_iota(jnp.int32, (tile, tile), 1)
    causal = s_idx < t_idx
    out = jnp.zeros((tile, LANES), F32)
    for hh in range(heads_per_block):
        own = lane_head == hh
        qh = jnp.where(own, q, jnp.zeros_like(q))
        bias = bias_ref[0, hp * heads_per_block + hh]

        def key_tile(kt):
            start = pl.multiple_of(kt * tile, tile)
            return k_ref[0, pl.ds(start, tile), :], v_ref[0, pl.ds(start, tile), :]

        ks, vs = key_tile(qi)
        carry, acc = _sb_tile(qh, ks, vs, bias, jnp.zeros((tile, 1), F32), jnp.zeros((tile, LANES), F32),
                              umat, causal)

        def body(j, state):
            ks_j, vs_j = key_tile(qi - j)
            return _sb_tile(qh, ks_j, vs_j, bias, state[0], state[1], umat, None)

        carry, acc = lax.fori_loop(1, qi + 1, body, (carry, acc))
        out = jnp.where(own, acc, out)
    o_ref[0] = out.astype(o_ref.dtype)


def _prompt_attention(q, k, v, sb_bias, *, head_dim):
    b, l, w = q.shape
    tile = min(ATT_TILE, l)
    qspec = pl.BlockSpec((1, tile, LANES), lambda bi, hp, qi: (bi, qi, hp))
    kvspec = pl.BlockSpec((1, l, LANES), lambda bi, hp, qi: (bi, 0, hp))
    return pl.pallas_call(
        functools.partial(_attn_kernel, tile=tile, head_dim=head_dim),
        grid=(b, w // LANES, l // tile),
        in_specs=[pl.BlockSpec(memory_space=pltpu.SMEM), qspec, kvspec, kvspec],
        out_specs=qspec,
        out_shape=jax.ShapeDtypeStruct((b, l, w), BF16),
        compiler_params=_cparams("parallel", "parallel", "arbitrary"),
        name="attn_prompt",
    )(sb_bias.reshape(1, -1).astype(F32), q, k, v)


def _sb_heads(qh, kh, vh, bias, carry, acc, umat, valid):
    h, r, _ = qh.shape
    s = kh.shape[1]
    z = jnp.einsum("hid,hsd->his", qh, kh, preferred_element_type=F32) + bias
    sp = _softplus(z)
    if valid is not None:
        sp = jnp.where(valid, sp, 0.0)
    hi, lo = _split_bf16(sp.reshape(h * r, s))
    suffix = (_dot(hi, umat) + _dot(lo, umat)).reshape(h, r, s)
    a = jnp.exp(z - sp - suffix - carry)
    if valid is not None:
        a = jnp.where(valid, a, 0.0)
    acc = acc + jnp.einsum("his,hsd->hid", a.astype(BF16), vh, preferred_element_type=F32)
    carry = carry + jnp.sum(sp, axis=2, keepdims=True)
    return carry, acc


def _sattn_kernel(pt_ref, q_ref, kn_ref, vn_ref, bias_ref, kc_ref, vc_ref, o_ref, carry_ref, acc_ref, *,
                  n_heads, page, n_new):
    del pt_ref
    j = pl.program_id(1)
    qh = q_ref[0]
    bias = bias_ref[...]
    rows = qh.shape[1]

    umat = _suffix_matrix(page)

    @pl.when(j == 0)
    def _():
        i_idx = lax.broadcasted_iota(jnp.int32, (1, rows, page), 1)
        s_idx = lax.broadcasted_iota(jnp.int32, (1, rows, page), 2)
        valid = (s_idx < i_idx) & (s_idx < n_new)
        carry, acc = _sb_heads(qh, kn_ref[0], vn_ref[0], bias,
                               jnp.zeros(carry_ref.shape, F32), jnp.zeros(acc_ref.shape, F32), umat, valid)
        carry_ref[...] = carry
        acc_ref[...] = acc

    def heads(ref):
        return jnp.stack([ref[pl.ds(h, page, stride=n_heads), :].astype(BF16) for h in range(n_heads)])

    carry, acc = _sb_heads(qh, heads(kc_ref), heads(vc_ref), bias, carry_ref[...], acc_ref[...], umat, None)
    carry_ref[...] = carry
    acc_ref[...] = acc

    @pl.when(j == pl.num_programs(1) - 1)
    def _():
        o_ref[0] = acc


def _sample_attention(q, k_new, v_new, cache_k, cache_v, page_table, sb_bias, *, n_new):
    b, h, r, dh = q.shape
    n_pool, page = cache_k.shape[0], cache_k.shape[1]
    new_b = pl.BlockSpec((1, h, page, dh), lambda bi, j, pt_ref: (bi, 0, 0, 0))
    n_pages = page_table.shape[1]
    rows_per_page = page * h
    kc = cache_k.reshape(n_pool * rows_per_page, dh)
    vc = cache_v.reshape(n_pool * rows_per_page, dh)
    pt = page_table.reshape(-1).astype(jnp.int32)
    bias = jnp.broadcast_to(sb_bias.astype(F32).reshape(h, 1, 1), (h, r, 1))
    per_b = pl.BlockSpec((1, h, r, dh), lambda bi, j, pt_ref: (bi, 0, 0, 0))
    paged = pl.BlockSpec((rows_per_page, dh), lambda bi, j, pt_ref: (pt_ref[bi * n_pages + n_pages - 1 - j], 0))
    return pl.pallas_call(
        functools.partial(_sattn_kernel, n_heads=h, page=page, n_new=n_new),
        grid_spec=pltpu.PrefetchScalarGridSpec(
            num_scalar_prefetch=1,
            grid=(b, n_pages),
            in_specs=[per_b, new_b, new_b,
                      pl.BlockSpec((h, r, 1), lambda bi, j, pt_ref: (0, 0, 0)),
                      paged, paged],
            out_specs=per_b,
            scratch_shapes=[pltpu.VMEM((h, r, 1), F32), pltpu.VMEM((h, r, dh), F32)]),
        out_shape=jax.ShapeDtypeStruct((b, h, r, dh), F32),
        compiler_params=_cparams("parallel", "arbitrary"),
        name="attn_sample",
    )(pt, q, k_new, v_new, bias, kc, vc)


def _ssm_kernel(u_ref, h0r_ref, h0i_ref, ar_ref, ai_ref, bre_ref, bim_ref, brel_ref, biml_ref,
                cre_ref, cim_ref, d_ref, y_ref, hr_ref, hi_ref, xr_ref, xi_ref, sr_ref, si_ref, *,
                batch, steps, precise):
    step = pl.program_id(0)
    n_chunks = bre_ref.shape[0]
    cw = bre_ref.shape[1]
    sw = bre_ref.shape[2]
    n_state = n_chunks * sw

    @pl.when(step == 0)
    def _():
        sr_ref[...] = h0r_ref[...]
        si_ref[...] = h0i_ref[...]

    u = u_ref[...]
    u_hi, u_lo = _split_bf16(u)
    for c in range(n_chunks):
        uc = u_hi[:, c * cw:(c + 1) * cw]
        br = _dot(uc, bre_ref[c])
        bi = _dot(uc, bim_ref[c])
        if precise:
            ul = u_lo[:, c * cw:(c + 1) * cw]
            br = br + _dot(ul, bre_ref[c]) + _dot(uc, brel_ref[c])
            bi = bi + _dot(ul, bim_ref[c]) + _dot(uc, biml_ref[c])
        xr_ref[:, c * sw:(c + 1) * sw] = br
        xi_ref[:, c * sw:(c + 1) * sw] = bi

    lanes = min(SSM_CHUNK, n_state)
    for c in range(n_state // lanes):
        sl = pl.ds(c * lanes, lanes)
        a_r = jnp.broadcast_to(ar_ref[:, sl], (batch, lanes))
        a_i = jnp.broadcast_to(ai_ref[:, sl], (batch, lanes))

        def body(t, state):
            x_r, x_i = state
            rows = pl.ds(pl.multiple_of(t * batch, batch), batch)
            n_r = a_r * x_r - a_i * x_i + xr_ref[rows, sl]
            n_i = a_r * x_i + a_i * x_r + xi_ref[rows, sl]
            xr_ref[rows, sl] = n_r
            xi_ref[rows, sl] = n_i
            return n_r, n_i

        x_r, x_i = lax.fori_loop(0, steps, body, (sr_ref[:, sl], si_ref[:, sl]))
        sr_ref[:, sl] = x_r
        si_ref[:, sl] = x_i

    for c in range(n_chunks):
        x_r = xr_ref[:, c * sw:(c + 1) * sw].astype(BF16)
        x_i = xi_ref[:, c * sw:(c + 1) * sw].astype(BF16)
        ch = slice(c * cw, (c + 1) * cw)
        y_ref[:, ch] = _dot(x_r, cre_ref[c]) - _dot(x_i, cim_ref[c]) + d_ref[:, ch] * u[:, ch]

    @pl.when(step == pl.num_programs(0) - 1)
    def _():
        hr_ref[...] = sr_ref[...]
        hi_ref[...] = si_ref[...]


def _ssm_params(a_re, a_im, log_dt, b_re, b_im, c_re, c_im):
    g, n = a_re.shape
    hg = b_re.shape[2]
    dt = jnp.exp(log_dt.astype(F32))[:, None]
    mag = jnp.exp(dt * a_re)
    abar_re = mag * jnp.cos(dt * a_im)
    abar_im = mag * jnp.sin(dt * a_im)
    den = a_re * a_re + a_im * a_im
    nr = abar_re - 1.0
    ni = abar_im
    coef_re = (nr * a_re + ni * a_im) / den
    coef_im = (ni * a_re - nr * a_im) / den
    bbar_re = coef_re[..., None] * b_re - coef_im[..., None] * b_im
    bbar_im = coef_re[..., None] * b_im + coef_im[..., None] * b_re
    gpc = LANES // hg
    nc = g // gpc
    eye = jnp.eye(gpc, dtype=F32)

    def pack_b(bb):
        blk = bb.reshape(nc, gpc, n, hg)
        return jnp.einsum("kgnc,gh->kgchn", blk, eye).reshape(nc, gpc * hg, gpc * n)

    def pack_c(cc):
        blk = cc.reshape(nc, gpc, hg, n)
        return jnp.einsum("kgcn,gh->kgnhc", blk, eye).reshape(nc, gpc * n, gpc * hg)

    pb_re, pb_im = pack_b(bbar_re), pack_b(bbar_im)
    bre_hi, bre_lo = _split_bf16(pb_re)
    bim_hi, bim_lo = _split_bf16(pb_im)
    return dict(ar=abar_re.reshape(1, g * n), ai=abar_im.reshape(1, g * n),
                bre=bre_hi, bim=bim_hi, brel=bre_lo, biml=bim_lo,
                cre=pack_c(c_re.astype(F32)).astype(BF16), cim=pack_c(c_im.astype(F32)).astype(BF16))


def _ssm(u_tb, h0_re, h0_im, sp, d_skip, *, batch, steps, precise):
    rows, w = u_tb.shape
    n_state = sp["ar"].shape[1]
    blk = steps * batch
    full = lambda a: _const_spec(a.shape)
    args = (sp["ar"], sp["ai"], sp["bre"], sp["bim"], sp["brel"], sp["biml"], sp["cre"], sp["cim"],
            d_skip.reshape(1, w).astype(F32))
    return pl.pallas_call(
        functools.partial(_ssm_kernel, batch=batch, steps=steps, precise=precise),
        grid=(rows // blk,),
        in_specs=[pl.BlockSpec((blk, w), lambda i: (i, 0)), full(h0_re), full(h0_im)] + [full(a) for a in args],
        out_specs=[pl.BlockSpec((blk, w), lambda i: (i, 0)),
                   pl.BlockSpec((batch, n_state), lambda i: (0, 0)),
                   pl.BlockSpec((batch, n_state), lambda i: (0, 0))],
        out_shape=[jax.ShapeDtypeStruct((rows, w), F32),
                   jax.ShapeDtypeStruct((batch, n_state), F32),
                   jax.ShapeDtypeStruct((batch, n_state), F32)],
        scratch_shapes=[pltpu.VMEM((blk, n_state), F32), pltpu.VMEM((blk, n_state), F32),
                        pltpu.VMEM((batch, n_state), F32), pltpu.VMEM((batch, n_state), F32)],
        compiler_params=_cparams("arbitrary"),
        name="ssm",
    )(u_tb, h0_re, h0_im, *args)


def _post_kernel(*refs, n_experts, aliased):
    if aliased:
        refs = refs[1:]
    (x_ref, y_ref, o_ref, g1_ref, sh_ref, sc_ref, wglu_ref, bglu_ref, gs_ref, ga_ref, wout_ref, n2_ref,
     wrh_ref, wrl_ref, rb_ref, x1_ref, h2_ref, idx_ref, gate_ref) = refs
    y = jax.nn.gelu(y_ref[...], approximate=True)
    glu = jax.nn.sigmoid(_dot(y.astype(BF16), wglu_ref[...]) + bglu_ref[...])
    n_ssm = _rms(y * glu, gs_ref[...])
    n_att = _rms(o_ref[...].astype(F32), ga_ref[...])
    merged = jnp.concatenate([n_ssm, n_att], axis=-1).astype(BF16)
    x1 = x_ref[...] + g1_ref[0] * _dot(merged, wout_ref[...])
    x1_ref[...] = x1
    h2 = _rms(x1, n2_ref[...]) * (1.0 + sc_ref[0]) + sh_ref[0]
    h2_ref[...] = h2

    h_hi, h_lo = _split_bf16(h2)
    logits = _dot(h_hi, wrh_ref[...]) + _dot(h_lo, wrh_ref[...]) + _dot(h_hi, wrl_ref[...])
    scores = jax.nn.sigmoid(logits)
    ranked = scores + rb_ref[...]
    tm = ranked.shape[0]
    e_lane = lax.broadcasted_iota(jnp.int32, (tm, n_experts), 1).astype(F32)
    o_lane = lax.broadcasted_iota(jnp.int32, (tm, IDX_LANES), 1)
    idx_out = jnp.zeros((tm, IDX_LANES), F32)
    sel_out = jnp.zeros((tm, IDX_LANES), F32)
    for k in range(TOP_K):
        best = jnp.max(ranked, axis=-1, keepdims=True)
        pick = jnp.min(jnp.where(ranked == best, e_lane, float(n_experts)), axis=-1, keepdims=True)
        chosen = e_lane == pick
        val = jnp.sum(jnp.where(chosen, scores, 0.0), axis=-1, keepdims=True)
        idx_out = jnp.where(o_lane == k, pick, idx_out)
        sel_out = jnp.where(o_lane == k, val, sel_out)
        ranked = jnp.where(chosen, -jnp.inf, ranked)
    idx_ref[...] = idx_out.astype(jnp.int32)
    gate_ref[...] = sel_out / jnp.sum(sel_out, axis=-1, keepdims=True) * ROUTED_SCALE


def _post(x2, y_src, y_spec, o_att, g1, sh2, sc2, pw, h2_prev, *, tm, tiles_per_group, h2_rows, row_offset):
    t, d = x2.shape
    w_att = o_att.shape[1]
    mrows = g1.shape[1]
    n_experts = pw["wr_hi"].shape[1]
    row = lambda i: (i, 0)
    grp = lambda i: (i // tiles_per_group, 0, 0)
    off = row_offset // tm
    weights = (pw["w_glu"], pw["b_glu"], pw["ssm_out_g"], pw["attn_out_g"], pw["w_out"], pw["norm2_g"],
               pw["wr_hi"], pw["wr_lo"], pw["router_bias"])
    in_specs = [pl.BlockSpec((tm, d), row), y_spec, pl.BlockSpec((tm, w_att), row),
                pl.BlockSpec((1, mrows, d), grp), pl.BlockSpec((1, mrows, d), grp), pl.BlockSpec((1, mrows, d), grp)]
    in_specs += [_const_spec(a.shape) for a in weights]
    args = (x2, y_src, o_att, g1, sh2, sc2) + weights
    aliases = {}
    aliased = h2_prev is not None
    if aliased:
        in_specs = [pl.BlockSpec(memory_space=pl.ANY)] + in_specs
        args = (h2_prev,) + args
        aliases = {0: 1}
    h2_shape = (h2_rows, d)
    return pl.pallas_call(
        functools.partial(_post_kernel, n_experts=n_experts, aliased=aliased),
        grid=(t // tm,),
        in_specs=in_specs,
        out_specs=[pl.BlockSpec((tm, d), row),
                   pl.BlockSpec((tm, d), lambda i: (i + off, 0)),
                   pl.BlockSpec((tm, IDX_LANES), row),
                   pl.BlockSpec((tm, IDX_LANES), row)],
        out_shape=[jax.ShapeDtypeStruct((t, d), F32),
                   jax.ShapeDtypeStruct(h2_shape, F32),
                   jax.ShapeDtypeStruct((t, IDX_LANES), jnp.int32),
                   jax.ShapeDtypeStruct((t, IDX_LANES), F32)],
        input_output_aliases=aliases,
        compiler_params=_cparams("parallel"),
        name="post",
    )(*args)


def _row_copy(src_hbm, dst, sem, row, r):
    return pltpu.make_async_copy(src_hbm.at[pl.ds(row, 1), :], dst.at[pl.ds(r, 1), :], sem)


def _gather_rows(idx_ref, src_hbm, dst, sem, n):
    def body(r, c):
        _row_copy(src_hbm, dst, sem, idx_ref[0, 0, r], r).start()
        return c
    lax.fori_loop(0, n, body, 0, unroll=8)


def _wait_rows(idx_ref, src_hbm, dst, sem, n):
    def body(r, c):
        _row_copy(src_hbm, dst, sem, idx_ref[0, 0, r], r).wait()
        return c
    lax.fori_loop(0, n, body, 0, unroll=8)


def _moe_kernel(be_ref, nused_ref, idx_ref, idxn_ref, h_hbm, wg_ref, wu_ref, wd_ref, y_ref, xbuf, sem):
    del be_ref
    i = pl.program_id(0)
    n_used = nused_ref[0]
    blk = xbuf.shape[1]
    slot = i % 2

    @pl.when(i == 0)
    def _():
        _gather_rows(idx_ref, h_hbm, xbuf.at[0], sem.at[0], blk)

    @pl.when(i + 1 < n_used)
    def _():
        _gather_rows(idxn_ref, h_hbm, xbuf.at[1 - slot], sem.at[1 - slot], blk)

    @pl.when(i < n_used)
    def _():
        _wait_rows(idx_ref, h_hbm, xbuf.at[slot], sem.at[slot], blk)
        x = xbuf[slot].astype(BF16)
        gate = _dot(x, wg_ref[0])
        up = _dot(x, wu_ref[0])
        act = (gate * jax.nn.sigmoid(gate) * up).astype(BF16)
        y_ref[...] = _dot(act, wd_ref[0])

    @pl.when(i >= n_used)
    def _():
        y_ref[...] = jnp.zeros(y_ref.shape, y_ref.dtype)


def _moe(h_all, rows_tok, block_expert, n_used, wg, wu, wd):
    n_blocks = block_expert.shape[0]
    blk = MOE_BLOCK
    d = h_all.shape[1]
    e, _, f = wg.shape
    idx3 = rows_tok.reshape(n_blocks, 1, blk)
    smem_blk = lambda im: pl.BlockSpec((1, 1, blk), im, memory_space=pltpu.SMEM)
    return pl.pallas_call(
        _moe_kernel,
        grid_spec=pltpu.PrefetchScalarGridSpec(
            num_scalar_prefetch=2,
            grid=(n_blocks,),
            in_specs=[smem_blk(lambda i, be, nu: (i, 0, 0)),
                      smem_blk(lambda i, be, nu: (jnp.minimum(i + 1, n_blocks - 1), 0, 0)),
                      pl.BlockSpec(memory_space=pl.ANY),
                      pl.BlockSpec((1, d, f), lambda i, be, nu: (be[i], 0, 0)),
                      pl.BlockSpec((1, d, f), lambda i, be, nu: (be[i], 0, 0)),
                      pl.BlockSpec((1, f, d), lambda i, be, nu: (be[i], 0, 0))],
            out_specs=pl.BlockSpec((blk, d), lambda i, be, nu: (i, 0)),
            scratch_shapes=[pltpu.VMEM((2, blk, d), F32), pltpu.SemaphoreType.DMA((2,))]),
        out_shape=jax.ShapeDtypeStruct((n_blocks * blk, d), F32),
        compiler_params=_cparams("arbitrary"),
        name="moe",
    )(block_expert, n_used, idx3, idx3, h_all, wg, wu, wd)


def _final_kernel(pos_ref, posn_ref, y_hbm, x1_ref, h2_ref, gate_ref, g2_ref, wsg_ref, wsu_ref, wsd_ref, nf_ref,
                  o_ref, ybuf, sem):
    i = pl.program_id(0)
    n = pl.num_programs(0)
    tm = x1_ref.shape[0]
    rows = ybuf.shape[1]
    slot = i % 2

    @pl.when(i == 0)
    def _():
        _gather_rows(pos_ref, y_hbm, ybuf.at[0], sem.at[0], rows)

    @pl.when(i + 1 < n)
    def _():
        _gather_rows(posn_ref, y_hbm, ybuf.at[1 - slot], sem.at[1 - slot], rows)

    h = h2_ref[...].astype(BF16)
    gate = _dot(h, wsg_ref[...])
    up = _dot(h, wsu_ref[...])
    moe = _dot((gate * jax.nn.sigmoid(gate) * up).astype(BF16), wsd_ref[...])

    _wait_rows(pos_ref, y_hbm, ybuf.at[slot], sem.at[slot], rows)
    gates = gate_ref[...]
    for k in range(TOP_K):
        moe = moe + gates[:, k:k + 1] * ybuf[slot, pl.ds(k * tm, tm), :]
    x = x1_ref[...] + g2_ref[0] * moe
    o_ref[...] = _rms(x, nf_ref[...])


def _final(y_rows, pos, x1, h2_all, gates, g2, fw, *, tm, tiles_per_group, row_offset):
    t, d = x1.shape
    n_tiles = t // tm
    mrows = g2.shape[1]
    pos3 = pos.reshape(n_tiles, tm, TOP_K).transpose(0, 2, 1).reshape(n_tiles, 1, TOP_K * tm)
    smem_blk = lambda im: pl.BlockSpec((1, 1, TOP_K * tm), im, memory_space=pltpu.SMEM)
    row = lambda i: (i, 0)
    off = row_offset // tm
    weights = (fw["ws_gate"], fw["ws_up"], fw["ws_down"], fw["normf_g"])
    return pl.pallas_call(
        _final_kernel,
        grid=(n_tiles,),
        in_specs=[smem_blk(lambda i: (i, 0, 0)),
                  smem_blk(lambda i: (jnp.minimum(i + 1, n_tiles - 1), 0, 0)),
                  pl.BlockSpec(memory_space=pl.ANY),
                  pl.BlockSpec((tm, d), row),
                  pl.BlockSpec((tm, d), lambda i: (i + off, 0)),
                  pl.BlockSpec((tm, IDX_LANES), row),
                  pl.BlockSpec((1, mrows, d), lambda i: (i // tiles_per_group, 0, 0))]
                 + [_const_spec(a.shape) for a in weights],
        out_specs=pl.BlockSpec((tm, d), row),
        out_shape=jax.ShapeDtypeStruct((t, d), F32),
        scratch_shapes=[pltpu.VMEM((2, TOP_K * tm, d), F32), pltpu.SemaphoreType.DMA((2,))],
        compiler_params=_cparams("arbitrary"),
        name="final",
    )(pos3, pos3, y_rows, x1, h2_all, gates, g2, *weights)


def _dispatch(idx, n_experts):
    t, k = idx.shape
    blk = MOE_BLOCK
    n_assign = t * k
    n_blocks = -(-n_assign // blk) + n_experts
    onehot = jnp.sum((idx[:, :, None] == jnp.arange(n_experts, dtype=jnp.int32)).astype(jnp.int32), axis=1)
    before = jnp.cumsum(onehot, axis=0) - onehot
    counts = jnp.sum(onehot, axis=0)
    padded = (counts + blk - 1) // blk * blk
    pad_end = jnp.cumsum(padded)
    pad_start = pad_end - padded
    pos = pad_start[idx] + jnp.take_along_axis(before, idx, axis=1)
    tok = jnp.broadcast_to(jnp.arange(t, dtype=jnp.int32)[:, None], (t, k))
    rows_tok = jnp.zeros((n_blocks * blk,), jnp.int32).at[pos.reshape(-1)].set(tok.reshape(-1))
    block_start = jnp.arange(n_blocks, dtype=jnp.int32) * blk
    block_expert = jnp.minimum(jnp.searchsorted(pad_end, block_start, side="right"), n_experts - 1).astype(jnp.int32)
    n_used = (pad_end[-1] // blk).astype(jnp.int32).reshape(1)
    return pos.astype(jnp.int32), rows_tok, block_expert, n_used


def kernel(x_prompt, x_sample, c_prompt, c_sample, cache_k, cache_v, state_ssm_re, state_ssm_im, page_table,
           norm1_g, norm2_g, w_ada, b_ada, w_in, sb_bias, ssm_a_re, ssm_a_im, ssm_log_dt, ssm_b_re, ssm_b_im,
           ssm_c_re, ssm_c_im, ssm_d, w_glu, b_glu, ssm_out_g, attn_out_g, w_out, w_router, router_bias,
           we_gate, we_up, we_down, ws_gate, ws_up, ws_down, normf_g):
    depth = w_in.shape[0]
    assert depth == 1, "single-layer trunk"
    bp, lp, d = x_prompt.shape
    bs, ls, _ = x_sample.shape
    n_heads, head_dim = cache_k.shape[3], cache_k.shape[4]
    w_att = n_heads * head_dim
    g_ssm, n_ssm = ssm_a_re.shape[1], ssm_a_re.shape[2]
    n_experts = w_router.shape[2]
    tp, ts = bp * lp, bs * ls
    t_all = tp + ts
    tm = min(256, lp)
    assert lp % tm == 0 and ts % 8 == 0 and tp % FINAL_TILE == 0 and ts % FINAL_TILE == 0

    c_all = jnp.concatenate([c_prompt, c_sample], axis=0)
    mod = _modulation(c_all, w_ada[0], b_ada[0]).reshape(bp + bs, N_MOD, d)
    mod_p = [mod[:bp, i][:, None, :] for i in range(N_MOD)]
    mod_s = [jnp.repeat(mod[bp:, i], ls, axis=0)[None] for i in range(N_MOD)]

    w_in_bf = w_in[0].astype(BF16)
    g1n = norm1_g[0].reshape(1, d)
    sp = _ssm_params(ssm_a_re[0].astype(F32), ssm_a_im[0].astype(F32), ssm_log_dt[0], ssm_b_re[0].astype(F32),
                     ssm_b_im[0].astype(F32), ssm_c_re[0], ssm_c_im[0])
    wr_hi, wr_lo = _split_bf16(w_router[0].astype(F32))
    pw = dict(w_glu=w_glu[0].astype(BF16), b_glu=b_glu[0].reshape(1, -1), ssm_out_g=ssm_out_g[0].reshape(1, -1),
              attn_out_g=attn_out_g[0].reshape(1, -1), w_out=w_out[0].astype(BF16), norm2_g=norm2_g[0].reshape(1, d),
              wr_hi=wr_hi, wr_lo=wr_lo, router_bias=router_bias[0].reshape(1, -1).astype(F32))
    fw = dict(ws_gate=ws_gate[0].astype(BF16), ws_up=ws_up[0].astype(BF16), ws_down=ws_down[0].astype(BF16),
              normf_g=normf_g.reshape(1, d))

    xp2 = x_prompt.reshape(tp, d)
    tiles_pb = lp // tm
    q_p, k_p, v_p, kb_p, vb_p, u_p = _in_proj(xp2, mod_p[0], mod_p[1], g1n, w_in_bf, w_att=w_att, head_dim=head_dim,
                                              tm=tm, tiles_per_group=tiles_pb, time_major_batches=bp)
    o_att_p = _prompt_attention(q_p.reshape(bp, lp, w_att), kb_p.reshape(bp, lp, w_att),
                                vb_p.reshape(bp, lp, w_att), sb_bias[0], head_dim=head_dim).reshape(tp, w_att)
    w_ssm = u_p.shape[1] // bp
    zeros_state = jnp.zeros((bp, g_ssm * n_ssm), F32)
    ssm_steps = max(1, min(lp, 256 // bp))
    y_p, hr_p, hi_p = _ssm(u_p.reshape(lp * bp, w_ssm), zeros_state, zeros_state, sp, ssm_d[0],
                           batch=bp, steps=ssm_steps, precise=False)
    y_p_spec = pl.BlockSpec((tm, w_ssm), lambda i: (i % tiles_pb, i // tiles_pb))
    x1_p, h2_all, idx_p, gate_p = _post(xp2, y_p.reshape(lp, bp * w_ssm), y_p_spec, o_att_p, mod_p[2], mod_p[3],
                                        mod_p[4], pw, None, tm=tm, tiles_per_group=tiles_pb, h2_rows=t_all,
                                        row_offset=0)

    xs2 = x_sample.reshape(ts, d)
    q_s, k_s, v_s, kb_s, vb_s, u_s = _in_proj(xs2, mod_s[0], mod_s[1], g1n, w_in_bf, w_att=w_att, head_dim=head_dim,
                                              tm=ts, tiles_per_group=1, time_major_batches=0)
    page = cache_k.shape[2]
    q_rows = 16

    def per_head(a, rows):
        a = a.reshape(bs, ls, n_heads, head_dim).transpose(0, 2, 1, 3)
        return jnp.pad(a, ((0, 0), (0, 0), (0, rows - ls), (0, 0)))

    o_s = _sample_attention(per_head(q_s, q_rows), per_head(kb_s, page), per_head(vb_s, page),
                            cache_k[0], cache_v[0], page_table, sb_bias[0], n_new=ls)
    o_att_s = o_s[:, :, :ls].transpose(0, 2, 1, 3).reshape(ts, w_att).astype(BF16)
    u_s_tb = u_s.reshape(bs, ls, w_ssm).transpose(1, 0, 2).reshape(ls * bs, w_ssm)
    y_s_tb, hr_s, hi_s = _ssm(u_s_tb, state_ssm_re[0].reshape(bs, -1).astype(F32),
                              state_ssm_im[0].reshape(bs, -1).astype(F32), sp, ssm_d[0],
                              batch=bs, steps=ls, precise=True)
    y_s = y_s_tb.reshape(ls, bs, w_ssm).transpose(1, 0, 2).reshape(ts, w_ssm)
    x1_s, h2_all, idx_s, gate_s = _post(xs2, y_s, pl.BlockSpec((ts, w_ssm), lambda i: (i, 0)), o_att_s, mod_s[2],
                                        mod_s[3], mod_s[4], pw, h2_all, tm=ts, tiles_per_group=1, h2_rows=t_all,
                                        row_offset=tp)

    idx_all = jnp.concatenate([idx_p[:, :TOP_K], idx_s[:, :TOP_K]], axis=0)
    pos, rows_tok, block_expert, n_used = _dispatch(idx_all, n_experts)
    y_rows = _moe(h2_all, rows_tok, block_expert, n_used,
                  we_gate[0].astype(BF16), we_up[0].astype(BF16), we_down[0].astype(BF16))
    ft = FINAL_TILE
    out_p = _final(y_rows, pos[:tp], x1_p, h2_all, gate_p, mod_p[5], fw, tm=ft, tiles_per_group=lp // ft,
                   row_offset=0)
    out_s = _final(y_rows, pos[tp:], x1_s, h2_all, gate_s, mod_s[5], fw, tm=ft, tiles_per_group=ts // ft,
                   row_offset=tp)

    kv_p = (1, bp, lp, n_heads, head_dim)
    kv_s = (1, bs, ls, n_heads, head_dim)
    st_p = (1, bp, g_ssm, n_ssm)
    st_s = (1, bs, g_ssm, n_ssm)
    return (out_p.reshape(bp, lp, d), out_s.reshape(bs, ls, d),
            k_p.reshape(kv_p), v_p.reshape(kv_p), hr_p.reshape(st_p), hi_p.reshape(st_p),
            k_s.reshape(kv_s), v_s.reshape(kv_s), hr_s.reshape(st_s), hi_s.reshape(st_s))
```

```python
import functools
import math

import jax
import jax.numpy as jnp
from jax import lax
from jax.experimental import pallas as pl
from jax.experimental.pallas import tpu as pltpu

F32 = jnp.float32
BF16 = jnp.bfloat16

RMS_EPS = 1e-6
TOP_K = 6
ROUTED_SCALE = 2.5
N_MOD = 6

LANES = 128
VMEM_LIMIT = 56 << 20
MOE_BLOCK = 256
FINAL_TILE = 128
ATT_Q_TILE = 512
ATT_K_TILE = 256
SSM_CHUNK = 512
IDX_LANES = 128
SAMPLE_PAGES_PER_STEP = 4
SAMPLE_Q_ROWS = 8


def _cparams(*sem):
    return pltpu.CompilerParams(dimension_semantics=sem, vmem_limit_bytes=VMEM_LIMIT)


def _const_spec(shape):
    nd = len(shape)
    return pl.BlockSpec(shape, lambda *_: (0,) * nd, pipeline_mode=pl.Buffered(1))


def _rms(x, g):
    inv = lax.rsqrt(jnp.mean(x * x, axis=-1, keepdims=True) + RMS_EPS)
    return x * inv * g


def _softplus(z):
    return jnp.maximum(z, 0.0) + jnp.log(1.0 + jnp.exp(-jnp.abs(z)))


def _split_bf16(x):
    hi = x.astype(BF16)
    lo = (x - hi.astype(F32)).astype(BF16)
    return hi, lo


def _dot(a, b):
    return jnp.dot(a, b, preferred_element_type=F32)


def _dot_nt(a, b):
    return lax.dot_general(a, b, (((1,), (1,)), ((), ())), preferred_element_type=F32)


def _suffix_matrix(n):
    j = lax.broadcasted_iota(jnp.int32, (n, n), 0)
    s = lax.broadcasted_iota(jnp.int32, (n, n), 1)
    return (j > s).astype(BF16)


def _mod_kernel(c_ref, w_ref, b_ref, o_ref):
    c = c_ref[...]
    a = (c * jax.nn.sigmoid(c)).astype(BF16)
    o_ref[...] = _dot(a, w_ref[...].astype(BF16)) + b_ref[...]


def _modulation(c, w_ada, b_ada):
    rows, d = c.shape
    n = w_ada.shape[1]
    tn = 1024
    return pl.pallas_call(
        _mod_kernel,
        grid=(n // tn,),
        in_specs=[_const_spec((rows, d)),
                  pl.BlockSpec((d, tn), lambda j: (0, j)),
                  pl.BlockSpec((1, tn), lambda j: (0, j))],
        out_specs=pl.BlockSpec((rows, tn), lambda j: (0, j)),
        out_shape=jax.ShapeDtypeStruct((rows, n), F32),
        compiler_params=_cparams("parallel"),
        name="mod",
    )(c, w_ada, b_ada.reshape(1, n))


def _inproj_kernel(x_ref, sh_ref, sc_ref, g_ref, w_ref, q_ref, k_ref, v_ref, kb_ref, vb_ref, u_ref, *, w_att, scale):
    x = x_ref[...]
    h = (_rms(x, g_ref[...]) * (1.0 + sc_ref[0]) + sh_ref[0]).astype(BF16)
    q = _dot(h, w_ref[:, 0:w_att])
    q_ref[...] = (q * scale).astype(BF16)
    k = _dot(h, w_ref[:, w_att:2 * w_att])
    k_ref[...] = k
    kb_ref[...] = k.astype(BF16)
    v = _dot(h, w_ref[:, 2 * w_att:3 * w_att])
    v_ref[...] = v
    vb_ref[...] = v.astype(BF16)
    u_ref[...] = _dot(h, w_ref[:, 3 * w_att:])


def _in_proj(x2, sh, sc, g, w_bf, *, w_att, head_dim, tm, tiles_per_group, time_major_batches):
    t, d = x2.shape
    n = w_bf.shape[1]
    w_ssm = n - 3 * w_att
    mrows = sh.shape[1]
    row = lambda i: (i, 0)
    grp = lambda i: (i // tiles_per_group, 0, 0)
    if time_major_batches:
        u_shape = (t // time_major_batches, time_major_batches * w_ssm)
        u_spec = pl.BlockSpec((tm, w_ssm), lambda i: (i % tiles_per_group, i // tiles_per_group))
    else:
        u_shape = (t, w_ssm)
        u_spec = pl.BlockSpec((tm, w_ssm), row)
    return pl.pallas_call(
        functools.partial(_inproj_kernel, w_att=w_att, scale=head_dim ** -0.5),
        grid=(t // tm,),
        in_specs=[pl.BlockSpec((tm, d), row),
                  pl.BlockSpec((1, mrows, d), grp),
                  pl.BlockSpec((1, mrows, d), grp),
                  _const_spec((1, d)),
                  _const_spec((d, n))],
        out_specs=[pl.BlockSpec((tm, w_att), row)] * 5 + [u_spec],
        out_shape=[jax.ShapeDtypeStruct((t, w_att), BF16),
                   jax.ShapeDtypeStruct((t, w_att), F32),
                   jax.ShapeDtypeStruct((t, w_att), F32),
                   jax.ShapeDtypeStruct((t, w_att), BF16),
                   jax.ShapeDtypeStruct((t, w_att), BF16),
                   jax.ShapeDtypeStruct(u_shape, F32)],
        compiler_params=_cparams("parallel"),
        name="in_proj",
    )(x2, sh, sc, g, w_bf)


def _sb_tile(qh, ks, vs, bias, carry, acc, umat, causal):
    z = _dot_nt(qh, ks) + bias
    sp = _softplus(z)
    if causal is not None:
        sp = jnp.where(causal, sp, 0.0)
    hi, lo = _split_bf16(sp)
    suffix = _dot(hi, umat) + _dot(lo, umat)
    a = jnp.exp(z - sp - suffix - carry)
    if causal is not None:
        a = jnp.where(causal, a, 0.0)
    acc = acc + _dot(a.astype(BF16), vs)
    carry = carry + jnp.sum(sp, axis=1, keepdims=True)
    return carry, acc


def _attn_kernel(bias_ref, q_ref, k_ref, v_ref, o_ref, *, tq, tk, head_dim):
    hp = pl.program_id(1)
    qi = pl.program_id(2)
    q = q_ref[0]
    heads_per_block = LANES // head_dim
    ratio = tq // tk
    lane_head = lax.broadcasted_iota(jnp.int32, (1, LANES), 1) // head_dim
    umat = _suffix_matrix(tk)
    t_idx = lax.broadcasted_iota(jnp.int32, (tq, tk), 0)
    s_idx = lax.broadcasted_iota(jnp.int32, (tq, tk), 1)
    owns = [lane_head == hh for hh in range(heads_per_block)]
    qhs = [jnp.where(own, q, jnp.zeros_like(q)) for own in owns]
    biases = [bias_ref[0, hp * heads_per_block + hh] for hh in range(heads_per_block)]

    def update(kt, state, causal):
        start = pl.multiple_of(kt * tk, tk)
        ks = k_ref[0, pl.ds(start, tk), :]
        vs = v_ref[0, pl.ds(start, tk), :]
        return tuple(_sb_tile(qhs[hh], ks, vs, biases[hh], state[hh][0], state[hh][1], umat, causal)
                     for hh in range(heads_per_block))

    state = tuple((jnp.zeros((tq, 1), F32), jnp.zeros((tq, LANES), F32)) for _ in range(heads_per_block))
    for r in reversed(range(ratio)):
        state = update(qi * ratio + r, state, (s_idx + r * tk) < t_idx)
    state = lax.fori_loop(0, qi * ratio, lambda j, st: update(qi * ratio - 1 - j, st, None), state)
    out = state[0][1]
    for hh in range(1, heads_per_block):
        out = jnp.where(owns[hh], state[hh][1], out)
    o_ref[0] = out.astype(o_ref.dtype)


def _prompt_attention(q, k, v, sb_bias, *, head_dim):
    b, l, w = q.shape
    tq = min(ATT_Q_TILE, l)
    tk = min(ATT_K_TILE, l)
    qspec = pl.BlockSpec((1, tq, LANES), lambda bi, hp, qi: (bi, qi, hp))
    kvspec = pl.BlockSpec((1, l, LANES), lambda bi, hp, qi: (bi, 0, hp))
    return pl.pallas_call(
        functools.partial(_attn_kernel, tq=tq, tk=tk, head_dim=head_dim),
        grid=(b, w // LANES, l // tq),
        in_specs=[pl.BlockSpec(memory_space=pltpu.SMEM), qspec, kvspec, kvspec],
        out_specs=qspec,
        out_shape=jax.ShapeDtypeStruct((b, l, w), BF16),
        compiler_params=_cparams("parallel", "parallel", "arbitrary"),
        name="attn_prompt",
    )(sb_bias.reshape(1, -1).astype(F32), q, k, v)


def _sb_pages(qbd, kts, vts, bias, carry, acc, umat, valid):
    page = kts[0].shape[1]
    z = _dot(qbd, jnp.concatenate(kts, axis=1)) + bias
    sp = _softplus(z)
    if valid is not None:
        sp = jnp.where(valid, sp, 0.0)
    probs = [None] * len(kts)
    for p in reversed(range(len(kts))):
        sl = slice(p * page, (p + 1) * page)
        sp_p = sp[:, sl]
        hi, lo = _split_bf16(sp_p)
        suffix = _dot(hi, umat) + _dot(lo, umat)
        probs[p] = jnp.exp(z[:, sl] - sp_p - suffix - carry)
        carry = carry + jnp.sum(sp_p, axis=1, keepdims=True)
    a = jnp.concatenate(probs, axis=1)
    if valid is not None:
        a = jnp.where(valid, a, 0.0)
    acc = acc + _dot_nt(jnp.concatenate(vts, axis=1), a.astype(BF16))
    return carry, acc


def _sattn_kernel(pt_ref, q_ref, kn_ref, vn_ref, bias_ref, *refs, pages_per_step, q_rows, n_new):
    del pt_ref
    k_refs = refs[:pages_per_step]
    v_refs = refs[pages_per_step:2 * pages_per_step]
    o_ref, carry_ref, acc_ref = refs[2 * pages_per_step:]
    j = pl.program_id(1)
    qbd = q_ref[0]
    bias = bias_ref[...]
    hr = qbd.shape[0]
    page = kn_ref.shape[2]
    umat = _suffix_matrix(page)

    @pl.when(j == 0)
    def _():
        i_idx = lax.broadcasted_iota(jnp.int32, (hr, page), 0) % q_rows
        s_idx = lax.broadcasted_iota(jnp.int32, (hr, page), 1)
        valid = (s_idx < i_idx) & (s_idx < n_new)
        carry, acc = _sb_pages(qbd, [kn_ref[0]], [vn_ref[0]], bias, jnp.zeros(carry_ref.shape, F32),
                               jnp.zeros(acc_ref.shape, F32), umat, valid)
        carry_ref[...] = carry
        acc_ref[...] = acc

    def flat(ref):
        x = ref[0]
        return x.reshape(x.shape[0] * x.shape[1], x.shape[2]).astype(BF16)

    carry, acc = _sb_pages(qbd, [flat(r) for r in k_refs], [flat(r) for r in v_refs], bias,
                           carry_ref[...], acc_ref[...], umat, None)
    carry_ref[...] = carry
    acc_ref[...] = acc

    @pl.when(j == pl.num_programs(1) - 1)
    def _():
        o_ref[0] = acc


def _sample_attention(q, k_new, v_new, cache_k, cache_v, page_table, sb_bias, *, batch):
    n_pool, page, h, dh = cache_k.shape
    hd = h * dh
    n_new = q.shape[0] // batch
    n_pages = page_table.shape[1]
    pps = math.gcd(SAMPLE_PAGES_PER_STEP, n_pages)
    hr = h * SAMPLE_Q_ROWS
    q4 = jnp.pad(q.reshape(batch, n_new, h, dh), ((0, 0), (0, SAMPLE_Q_ROWS - n_new), (0, 0), (0, 0)))
    qbd = jnp.einsum("bihd,gh->bgihd", q4, jnp.eye(h, dtype=q.dtype)).reshape(batch, hr, hd)

    def new_t(a):
        a = a.reshape(batch, n_new, hd).transpose(0, 2, 1)
        return jnp.pad(a, ((0, 0), (0, 0), (0, page - n_new)))

    kc = cache_k.transpose(0, 2, 3, 1)
    vc = cache_v.transpose(0, 2, 3, 1)
    pt = page_table.reshape(-1).astype(jnp.int32)
    bias = jnp.repeat(sb_bias.astype(F32), SAMPLE_Q_ROWS).reshape(hr, 1)

    def paged(p):
        return pl.BlockSpec((1, h, dh, page),
                            lambda bi, j, pt_ref: (pt_ref[bi * n_pages + n_pages - pps * (j + 1) + p], 0, 0, 0))

    new_b = pl.BlockSpec((1, hd, page), lambda bi, j, pt_ref: (bi, 0, 0))
    out = pl.pallas_call(
        functools.partial(_sattn_kernel, pages_per_step=pps, q_rows=SAMPLE_Q_ROWS, n_new=n_new),
        grid_spec=pltpu.PrefetchScalarGridSpec(
            num_scalar_prefetch=1,
            grid=(batch, n_pages // pps),
            in_specs=[pl.BlockSpec((1, hr, hd), lambda bi, j, pt_ref: (bi, 0, 0)), new_b, new_b,
                      pl.BlockSpec((hr, 1), lambda bi, j, pt_ref: (0, 0))]
                     + [paged(p) for p in range(pps)] * 2,
            out_specs=pl.BlockSpec((1, hd, hr), lambda bi, j, pt_ref: (bi, 0, 0)),
            scratch_shapes=[pltpu.VMEM((hr, 1), F32), pltpu.VMEM((hd, hr), F32)]),
        out_shape=jax.ShapeDtypeStruct((batch, hd, hr), F32),
        compiler_params=_cparams("parallel", "arbitrary"),
        name="attn_sample",
    )(pt, qbd, new_t(k_new), new_t(v_new), bias, *([kc] * pps), *([vc] * pps))
    o5 = out.reshape(batch, h, dh, h, SAMPLE_Q_ROWS)
    diag = jnp.einsum("bhdhi->bihd", o5)
    return diag[:, :n_new].reshape(batch * n_new, hd)


def _ssm_kernel(u_ref, h0r_ref, h0i_ref, ar_ref, ai_ref, bre_ref, bim_ref, brel_ref, biml_ref,
                cre_ref, cim_ref, d_ref, y_ref, hr_ref, hi_ref, xr_ref, xi_ref, sr_ref, si_ref, *,
                batch, steps, precise):
    step = pl.program_id(0)
    n_chunks = bre_ref.shape[0]
    cw = bre_ref.shape[1]
    sw = bre_ref.shape[2]
    n_state = n_chunks * sw

    @pl.when(step == 0)
    def _():
        sr_ref[...] = h0r_ref[...]
        si_ref[...] = h0i_ref[...]

    u = u_ref[...]
    u_hi, u_lo = _split_bf16(u)
    for c in range(n_chunks):
        uc = u_hi[:, c * cw:(c + 1) * cw]
        br = _dot(uc, bre_ref[c])
        bi = _dot(uc, bim_ref[c])
        if precise:
            ul = u_lo[:, c * cw:(c + 1) * cw]
            br = br + _dot(ul, bre_ref[c]) + _dot(uc, brel_ref[c])
            bi = bi + _dot(ul, bim_ref[c]) + _dot(uc, biml_ref[c])
        xr_ref[:, c * sw:(c + 1) * sw] = br
        xi_ref[:, c * sw:(c + 1) * sw] = bi

    lanes = min(SSM_CHUNK, n_state)
    for c in range(n_state // lanes):
        sl = pl.ds(c * lanes, lanes)
        a_r = jnp.broadcast_to(ar_ref[:, sl], (batch, lanes))
        a_i = jnp.broadcast_to(ai_ref[:, sl], (batch, lanes))

        def body(t, state):
            x_r, x_i = state
            rows = pl.ds(pl.multiple_of(t * batch, batch), batch)
            n_r = a_r * x_r - a_i * x_i + xr_ref[rows, sl]
            n_i = a_r * x_i + a_i * x_r + xi_ref[rows, sl]
            xr_ref[rows, sl] = n_r
            xi_ref[rows, sl] = n_i
            return n_r, n_i

        x_r, x_i = lax.fori_loop(0, steps, body, (sr_ref[:, sl], si_ref[:, sl]))
        sr_ref[:, sl] = x_r
        si_ref[:, sl] = x_i

    for c in range(n_chunks):
        x_r = xr_ref[:, c * sw:(c + 1) * sw].astype(BF16)
        x_i = xi_ref[:, c * sw:(c + 1) * sw].astype(BF16)
        ch = slice(c * cw, (c + 1) * cw)
        y_ref[:, ch] = _dot(x_r, cre_ref[c]) - _dot(x_i, cim_ref[c]) + d_ref[:, ch] * u[:, ch]

    @pl.when(step == pl.num_programs(0) - 1)
    def _():
        hr_ref[...] = sr_ref[...]
        hi_ref[...] = si_ref[...]


def _ssm_params(a_re, a_im, log_dt, b_re, b_im, c_re, c_im):
    g, n = a_re.shape
    hg = b_re.shape[2]
    dt = jnp.exp(log_dt.astype(F32))[:, None]
    mag = jnp.exp(dt * a_re)
    abar_re = mag * jnp.cos(dt * a_im)
    abar_im = mag * jnp.sin(dt * a_im)
    den = a_re * a_re + a_im * a_im
    nr = abar_re - 1.0
    ni = abar_im
    coef_re = (nr * a_re + ni * a_im) / den
    coef_im = (ni * a_re - nr * a_im) / den
    bbar_re = coef_re[..., None] * b_re - coef_im[..., None] * b_im
    bbar_im = coef_re[..., None] * b_im + coef_im[..., None] * b_re
    gpc = LANES // hg
    nc = g // gpc
    eye = jnp.eye(gpc, dtype=F32)

    def pack_b(bb):
        blk = bb.reshape(nc, gpc, n, hg)
        return jnp.einsum("kgnc,gh->kgchn", blk, eye).reshape(nc, gpc * hg, gpc * n)

    def pack_c(cc):
        blk = cc.reshape(nc, gpc, hg, n)
        return jnp.einsum("kgcn,gh->kgnhc", blk, eye).reshape(nc, gpc * n, gpc * hg)

    pb_re, pb_im = pack_b(bbar_re), pack_b(bbar_im)
    bre_hi, bre_lo = _split_bf16(pb_re)
    bim_hi, bim_lo = _split_bf16(pb_im)
    return dict(ar=abar_re.reshape(1, g * n), ai=abar_im.reshape(1, g * n),
                bre=bre_hi, bim=bim_hi, brel=bre_lo, biml=bim_lo,
                cre=pack_c(c_re.astype(F32)).astype(BF16), cim=pack_c(c_im.astype(F32)).astype(BF16))


def _ssm(u_tb, h0_re, h0_im, sp, d_skip, *, batch, steps, precise):
    rows, w = u_tb.shape
    n_state = sp["ar"].shape[1]
    blk = steps * batch
    full = lambda a: _const_spec(a.shape)
    args = (sp["ar"], sp["ai"], sp["bre"], sp["bim"], sp["brel"], sp["biml"], sp["cre"], sp["cim"],
            d_skip.reshape(1, w).astype(F32))
    return pl.pallas_call(
        functools.partial(_ssm_kernel, batch=batch, steps=steps, precise=precise),
        grid=(rows // blk,),
        in_specs=[pl.BlockSpec((blk, w), lambda i: (i, 0)), full(h0_re), full(h0_im)] + [full(a) for a in args],
        out_specs=[pl.BlockSpec((blk, w), lambda i: (i, 0)),
                   pl.BlockSpec((batch, n_state), lambda i: (0, 0)),
                   pl.BlockSpec((batch, n_state), lambda i: (0, 0))],
        out_shape=[jax.ShapeDtypeStruct((rows, w), F32),
                   jax.ShapeDtypeStruct((batch, n_state), F32),
                   jax.ShapeDtypeStruct((batch, n_state), F32)],
        scratch_shapes=[pltpu.VMEM((blk, n_state), F32), pltpu.VMEM((blk, n_state), F32),
                        pltpu.VMEM((batch, n_state), F32), pltpu.VMEM((batch, n_state), F32)],
        compiler_params=_cparams("arbitrary"),
        name="ssm",
    )(u_tb, h0_re, h0_im, *args)


def _post_kernel(*refs, n_experts, aliased):
    if aliased:
        refs = refs[1:]
    (x_ref, y_ref, o_ref, g1_ref, sh_ref, sc_ref, wglu_ref, bglu_ref, gs_ref, ga_ref, wout_ref, n2_ref,
     wrh_ref, wrl_ref, rb_ref, x1_ref, h2_ref, idx_ref, gate_ref) = refs
    y = jax.nn.gelu(y_ref[...], approximate=True)
    glu = jax.nn.sigmoid(_dot(y.astype(BF16), wglu_ref[...]) + bglu_ref[...])
    n_ssm = _rms(y * glu, gs_ref[...])
    n_att = _rms(o_ref[...].astype(F32), ga_ref[...])
    merged = jnp.concatenate([n_ssm, n_att], axis=-1).astype(BF16)
    x1 = x_ref[...] + g1_ref[0] * _dot(merged, wout_ref[...])
    x1_ref[...] = x1
    h2 = _rms(x1, n2_ref[...]) * (1.0 + sc_ref[0]) + sh_ref[0]
    h2_ref[...] = h2

    h_hi, h_lo = _split_bf16(h2)
    logits = _dot(h_hi, wrh_ref[...]) + _dot(h_lo, wrh_ref[...]) + _dot(h_hi, wrl_ref[...])
    scores = jax.nn.sigmoid(logits)
    ranked = scores + rb_ref[...]
    tm = ranked.shape[0]
    e_lane = lax.broadcasted_iota(jnp.int32, (tm, n_experts), 1).astype(F32)
    o_lane = lax.broadcasted_iota(jnp.int32, (tm, IDX_LANES), 1)
    idx_out = jnp.zeros((tm, IDX_LANES), F32)
    sel_out = jnp.zeros((tm, IDX_LANES), F32)
    for k in range(TOP_K):
        best = jnp.max(ranked, axis=-1, keepdims=True)
        pick = jnp.min(jnp.where(ranked == best, e_lane, float(n_experts)), axis=-1, keepdims=True)
        chosen = e_lane == pick
        val = jnp.sum(jnp.where(chosen, scores, 0.0), axis=-1, keepdims=True)
        idx_out = jnp.where(o_lane == k, pick, idx_out)
        sel_out = jnp.where(o_lane == k, val, sel_out)
        ranked = jnp.where(chosen, -jnp.inf, ranked)
    idx_ref[...] = idx_out.astype(jnp.int32)
    gate_ref[...] = sel_out / jnp.sum(sel_out, axis=-1, keepdims=True) * ROUTED_SCALE


def _post(x2, y_src, y_spec, o_att, g1, sh2, sc2, pw, h2_prev, *, tm, tiles_per_group, h2_rows, row_offset):
    t, d = x2.shape
    w_att = o_att.shape[1]
    mrows = g1.shape[1]
    n_experts = pw["wr_hi"].shape[1]
    row = lambda i: (i, 0)
    grp = lambda i: (i // tiles_per_group, 0, 0)
    off = row_offset // tm
    weights = (pw["w_glu"], pw["b_glu"], pw["ssm_out_g"], pw["attn_out_g"], pw["w_out"], pw["norm2_g"],
               pw["wr_hi"], pw["wr_lo"], pw["router_bias"])
    in_specs = [pl.BlockSpec((tm, d), row), y_spec, pl.BlockSpec((tm, w_att), row),
                pl.BlockSpec((1, mrows, d), grp), pl.BlockSpec((1, mrows, d), grp), pl.BlockSpec((1, mrows, d), grp)]
    in_specs += [_const_spec(a.shape) for a in weights]
    args = (x2, y_src, o_att, g1, sh2, sc2) + weights
    aliases = {}
    aliased = h2_prev is not None
    if aliased:
        in_specs = [pl.BlockSpec(memory_space=pl.ANY)] + in_specs
        args = (h2_prev,) + args
        aliases = {0: 1}
    h2_shape = (h2_rows, d)
    return pl.pallas_call(
        functools.partial(_post_kernel, n_experts=n_experts, aliased=aliased),
        grid=(t // tm,),
        in_specs=in_specs,
        out_specs=[pl.BlockSpec((tm, d), row),
                   pl.BlockSpec((tm, d), lambda i: (i + off, 0)),
                   pl.BlockSpec((tm, IDX_LANES), row),
                   pl.BlockSpec((tm, IDX_LANES), row)],
        out_shape=[jax.ShapeDtypeStruct((t, d), F32),
                   jax.ShapeDtypeStruct(h2_shape, F32),
                   jax.ShapeDtypeStruct((t, IDX_LANES), jnp.int32),
                   jax.ShapeDtypeStruct((t, IDX_LANES), F32)],
        input_output_aliases=aliases,
        compiler_params=_cparams("parallel"),
        name="post",
    )(*args)


def _row_copy(src_hbm, dst, sem, row, r):
    return pltpu.make_async_copy(src_hbm.at[pl.ds(row, 1), :], dst.at[pl.ds(r, 1), :], sem)


def _gather_rows(idx_ref, src_hbm, dst, sem, n):
    for r in range(n):
        _row_copy(src_hbm, dst, sem, idx_ref[0, 0, r], r).start()


def _wait_rows(src_hbm, dst, sem):
    pltpu.make_async_copy(src_hbm.at[pl.ds(0, dst.shape[0]), :], dst, sem).wait()


def _moe_kernel(be_ref, nused_ref, idx_ref, idxn_ref, h_hbm, wg_ref, wu_ref, wd_ref, y_ref, xbuf0, xbuf1, sem, *,
                blk):
    del be_ref
    i = pl.program_id(0)
    n_used = nused_ref[0]
    bufs = (xbuf0, xbuf1)

    @pl.when(i == 0)
    def _():
        _gather_rows(idx_ref, h_hbm, xbuf0, sem.at[0], blk)

    def step(cur):
        nxt = 1 - cur
        _wait_rows(h_hbm, bufs[cur], sem.at[cur])
        _gather_rows(idxn_ref, h_hbm, bufs[nxt], sem.at[nxt], blk)
        x = bufs[cur][...].astype(BF16)
        gate = _dot(x, wg_ref[0])
        up = _dot(x, wu_ref[0])
        act = (gate * jax.nn.sigmoid(gate) * up).astype(BF16)
        y_ref[...] = _dot(act, wd_ref[0])

        @pl.when(i == n_used - 1)
        def _():
            _wait_rows(h_hbm, bufs[nxt], sem.at[nxt])

    for cur in range(2):
        pl.when(jnp.logical_and(i < n_used, i % 2 == cur))(functools.partial(step, cur))

    @pl.when(i >= n_used)
    def _():
        y_ref[...] = jnp.zeros(y_ref.shape, y_ref.dtype)


def _moe(h_all, rows_tok, block_expert, n_used, wg, wu, wd):
    n_blocks = block_expert.shape[0]
    blk = MOE_BLOCK
    e, d, f = wg.shape
    idx3 = rows_tok.reshape(n_blocks, 1, blk)
    smem_blk = lambda im: pl.BlockSpec((1, 1, blk), im, memory_space=pltpu.SMEM)
    return pl.pallas_call(
        functools.partial(_moe_kernel, blk=blk),
        grid_spec=pltpu.PrefetchScalarGridSpec(
            num_scalar_prefetch=2,
            grid=(n_blocks,),
            in_specs=[smem_blk(lambda i, be, nu: (i, 0, 0)),
                      smem_blk(lambda i, be, nu: (jnp.minimum(i + 1, n_blocks - 1), 0, 0)),
                      pl.BlockSpec(memory_space=pl.ANY),
                      pl.BlockSpec((1, d, f), lambda i, be, nu: (be[i], 0, 0)),
                      pl.BlockSpec((1, d, f), lambda i, be, nu: (be[i], 0, 0)),
                      pl.BlockSpec((1, f, d), lambda i, be, nu: (be[i], 0, 0))],
            out_specs=pl.BlockSpec((blk, d), lambda i, be, nu: (i, 0)),
            scratch_shapes=[pltpu.VMEM((blk, d), F32), pltpu.VMEM((blk, d), F32), pltpu.SemaphoreType.DMA((2,))]),
        out_shape=jax.ShapeDtypeStruct((n_blocks * blk, d), F32),
        compiler_params=_cparams("arbitrary"),
        name="moe",
    )(block_expert, n_used, idx3, idx3, h_all, wg, wu, wd)


def _final_kernel(pos_ref, posn_ref, y_hbm, x1_ref, h2_ref, gate_ref, g2_ref, wsg_ref, wsu_ref, wsd_ref, nf_ref,
                  o_ref, ybuf0, ybuf1, sem):
    i = pl.program_id(0)
    n = pl.num_programs(0)
    tm = x1_ref.shape[0]
    rows = TOP_K * tm
    bufs = (ybuf0, ybuf1)

    @pl.when(i == 0)
    def _():
        _gather_rows(pos_ref, y_hbm, ybuf0, sem.at[0], rows)

    def step(cur):
        nxt = 1 - cur
        _wait_rows(y_hbm, bufs[cur], sem.at[cur])
        _gather_rows(posn_ref, y_hbm, bufs[nxt], sem.at[nxt], rows)
        gates = gate_ref[...]
        routed = gates[:, 0:1] * bufs[cur][pl.ds(0, tm), :]
        for k in range(1, TOP_K):
            routed = routed + gates[:, k:k + 1] * bufs[cur][pl.ds(k * tm, tm), :]
        h = h2_ref[...].astype(BF16)
        gate = _dot(h, wsg_ref[...])
        up = _dot(h, wsu_ref[...])
        shared = _dot((gate * jax.nn.sigmoid(gate) * up).astype(BF16), wsd_ref[...])
        x = x1_ref[...] + g2_ref[0] * (routed + shared)
        o_ref[...] = _rms(x, nf_ref[...])

        @pl.when(i == n - 1)
        def _():
            _wait_rows(y_hbm, bufs[nxt], sem.at[nxt])

    for cur in range(2):
        pl.when(i % 2 == cur)(functools.partial(step, cur))


def _final(y_rows, pos, x1, h2_all, gates, g2, fw, *, tm, tiles_per_group, row_offset):
    t, d = x1.shape
    n_tiles = t // tm
    mrows = g2.shape[1]
    pos3 = pos.reshape(n_tiles, tm, TOP_K).transpose(0, 2, 1).reshape(n_tiles, 1, TOP_K * tm)
    smem_blk = lambda im: pl.BlockSpec((1, 1, TOP_K * tm), im, memory_space=pltpu.SMEM)
    row = lambda i: (i, 0)
    off = row_offset // tm
    weights = (fw["ws_gate"], fw["ws_up"], fw["ws_down"], fw["normf_g"])
    return pl.pallas_call(
        _final_kernel,
        grid=(n_tiles,),
        in_specs=[smem_blk(lambda i: (i, 0, 0)),
                  smem_blk(lambda i: (jnp.minimum(i + 1, n_tiles - 1), 0, 0)),
                  pl.BlockSpec(memory_space=pl.ANY),
                  pl.BlockSpec((tm, d), row),
                  pl.BlockSpec((tm, d), lambda i: (i + off, 0)),
                  pl.BlockSpec((tm, IDX_LANES), row),
                  pl.BlockSpec((1, mrows, d), lambda i: (i // tiles_per_group, 0, 0))]
                 + [_const_spec(a.shape) for a in weights],
        out_specs=pl.BlockSpec((tm, d), row),
        out_shape=jax.ShapeDtypeStruct((t, d), F32),
        scratch_shapes=[pltpu.VMEM((TOP_K * tm, d), F32), pltpu.VMEM((TOP_K * tm, d), F32),
                        pltpu.SemaphoreType.DMA((2,))],
        compiler_params=_cparams("arbitrary"),
        name="final",
    )(pos3, pos3, y_rows, x1, h2_all, gates, g2, *weights)


def _dispatch(idx, n_experts):
    t, k = idx.shape
    blk = MOE_BLOCK
    n_assign = t * k
    n_blocks = -(-n_assign // blk) + n_experts
    onehot = jnp.sum((idx[:, :, None] == jnp.arange(n_experts, dtype=jnp.int32)).astype(jnp.int32), axis=1)
    before = jnp.cumsum(onehot, axis=0) - onehot
    counts = jnp.sum(onehot, axis=0)
    padded = (counts + blk - 1) // blk * blk
    pad_end = jnp.cumsum(padded)
    pad_start = pad_end - padded
    pos = pad_start[idx] + jnp.take_along_axis(before, idx, axis=1)
    tok = jnp.broadcast_to(jnp.arange(t, dtype=jnp.int32)[:, None], (t, k))
    rows_tok = jnp.zeros((n_blocks * blk,), jnp.int32).at[pos.reshape(-1)].set(tok.reshape(-1))
    block_start = jnp.arange(n_blocks, dtype=jnp.int32) * blk
    owner = jnp.sum((pad_end[None, :] <= block_start[:, None]).astype(jnp.int32), axis=1)
    block_expert = jnp.minimum(owner, n_experts - 1).astype(jnp.int32)
    n_used = (pad_end[-1] // blk).astype(jnp.int32).reshape(1)
    return pos.astype(jnp.int32), rows_tok, block_expert, n_used


def kernel(x_prompt, x_sample, c_prompt, c_sample, cache_k, cache_v, state_ssm_re, state_ssm_im, page_table,
           norm1_g, norm2_g, w_ada, b_ada, w_in, sb_bias, ssm_a_re, ssm_a_im, ssm_log_dt, ssm_b_re, ssm_b_im,
           ssm_c_re, ssm_c_im, ssm_d, w_glu, b_glu, ssm_out_g, attn_out_g, w_out, w_router, router_bias,
           we_gate, we_up, we_down, ws_gate, ws_up, ws_down, normf_g):
    depth = w_in.shape[0]
    assert depth == 1, "single-layer trunk"
    bp, lp, d = x_prompt.shape
    bs, ls, _ = x_sample.shape
    n_heads, head_dim = cache_k.shape[3], cache_k.shape[4]
    w_att = n_heads * head_dim
    g_ssm, n_ssm = ssm_a_re.shape[1], ssm_a_re.shape[2]
    n_experts = w_router.shape[2]
    tp, ts = bp * lp, bs * ls
    t_all = tp + ts
    tm = min(256, lp)
    assert lp % tm == 0 and ts % 8 == 0 and tp % FINAL_TILE == 0 and ts % FINAL_TILE == 0

    c_all = jnp.concatenate([c_prompt, c_sample], axis=0)
    mod = _modulation(c_all, w_ada[0], b_ada[0]).reshape(bp + bs, N_MOD, d)
    mod_p = [mod[:bp, i][:, None, :] for i in range(N_MOD)]
    mod_s = [jnp.repeat(mod[bp:, i], ls, axis=0)[None] for i in range(N_MOD)]

    w_in_bf = w_in[0].astype(BF16)
    g1n = norm1_g[0].reshape(1, d)
    sp = _ssm_params(ssm_a_re[0].astype(F32), ssm_a_im[0].astype(F32), ssm_log_dt[0], ssm_b_re[0].astype(F32),
                     ssm_b_im[0].astype(F32), ssm_c_re[0], ssm_c_im[0])
    wr_hi, wr_lo = _split_bf16(w_router[0].astype(F32))
    pw = dict(w_glu=w_glu[0].astype(BF16), b_glu=b_glu[0].reshape(1, -1), ssm_out_g=ssm_out_g[0].reshape(1, -1),
              attn_out_g=attn_out_g[0].reshape(1, -1), w_out=w_out[0].astype(BF16), norm2_g=norm2_g[0].reshape(1, d),
              wr_hi=wr_hi, wr_lo=wr_lo, router_bias=router_bias[0].reshape(1, -1).astype(F32))
    fw = dict(ws_gate=ws_gate[0].astype(BF16), ws_up=ws_up[0].astype(BF16), ws_down=ws_down[0].astype(BF16),
              normf_g=normf_g.reshape(1, d))

    xp2 = x_prompt.reshape(tp, d)
    tiles_pb = lp // tm
    q_p, k_p, v_p, kb_p, vb_p, u_p = _in_proj(xp2, mod_p[0], mod_p[1], g1n, w_in_bf, w_att=w_att, head_dim=head_dim,
                                              tm=tm, tiles_per_group=tiles_pb, time_major_batches=bp)
    o_att_p = _prompt_attention(q_p.reshape(bp, lp, w_att), kb_p.reshape(bp, lp, w_att),
                                vb_p.reshape(bp, lp, w_att), sb_bias[0], head_dim=head_dim).reshape(tp, w_att)
    w_ssm = u_p.shape[1] // bp
    zeros_state = jnp.zeros((bp, g_ssm * n_ssm), F32)
    ssm_steps = max(1, min(lp, 256 // bp))
    y_p, hr_p, hi_p = _ssm(u_p.reshape(lp * bp, w_ssm), zeros_state, zeros_state, sp, ssm_d[0],
                           batch=bp, steps=ssm_steps, precise=False)
    y_p_spec = pl.BlockSpec((tm, w_ssm), lambda i: (i % tiles_pb, i // tiles_pb))
    x1_p, h2_all, idx_p, gate_p = _post(xp2, y_p.reshape(lp, bp * w_ssm), y_p_spec, o_att_p, mod_p[2], mod_p[3],
                                        mod_p[4], pw, None, tm=tm, tiles_per_group=tiles_pb, h2_rows=t_all,
                                        row_offset=0)

    xs2 = x_sample.reshape(ts, d)
    q_s, k_s, v_s, kb_s, vb_s, u_s = _in_proj(xs2, mod_s[0], mod_s[1], g1n, w_in_bf, w_att=w_att, head_dim=head_dim,
                                              tm=ts, tiles_per_group=1, time_major_batches=0)
    assert ls <= SAMPLE_Q_ROWS
    o_att_s = _sample_attention(q_s, kb_s, vb_s, cache_k[0], cache_v[0], page_table, sb_bias[0],
                                batch=bs).astype(BF16)
    u_s_tb = u_s.reshape(bs, ls, w_ssm).transpose(1, 0, 2).reshape(ls * bs, w_ssm)
    y_s_tb, hr_s, hi_s = _ssm(u_s_tb, state_ssm_re[0].reshape(bs, -1).astype(F32),
                              state_ssm_im[0].reshape(bs, -1).astype(F32), sp, ssm_d[0],
                              batch=bs, steps=ls, precise=True)
    y_s = y_s_tb.reshape(ls, bs, w_ssm).transpose(1, 0, 2).reshape(ts, w_ssm)
    x1_s, h2_all, idx_s, gate_s = _post(xs2, y_s, pl.BlockSpec((ts, w_ssm), lambda i: (i, 0)), o_att_s, mod_s[2],
                                        mod_s[3], mod_s[4], pw, h2_all, tm=ts, tiles_per_group=1, h2_rows=t_all,
                                        row_offset=tp)

    idx_all = jnp.concatenate([idx_p[:, :TOP_K], idx_s[:, :TOP_K]], axis=0)
    pos, rows_tok, block_expert, n_used = _dispatch(idx_all, n_experts)
    y_rows = _moe(h2_all, rows_tok, block_expert, n_used,
                  we_gate[0].astype(BF16), we_up[0].astype(BF16), we_down[0].astype(BF16))
    ft = FINAL_TILE
    out_p = _final(y_rows, pos[:tp], x1_p, h2_all, gate_p, mod_p[5], fw, tm=ft, tiles_per_group=lp // ft,
                   row_offset=0)
    out_s = _final(y_rows, pos[tp:], x1_s, h2_all, gate_s, mod_s[5], fw, tm=ft, tiles_per_group=ts // ft,
                   row_offset=tp)

    kv_p = (1, bp, lp, n_heads, head_dim)
    kv_s = (1, bs, ls, n_heads, head_dim)
    st_p = (1, bp, g_ssm, n_ssm)
    st_s = (1, bs, g_ssm, n_ssm)
    return (out_p.reshape(bp, lp, d), out_s.reshape(bs, ls, d),
            k_p.reshape(kv_p), v_p.reshape(kv_p), hr_p.reshape(st_p), hi_p.reshape(st_p),
            k_s.reshape(kv_s), v_s.reshape(kv_s), hr_s.reshape(st_s), hi_s.reshape(st_s))
```

```python
import functools
import math

import jax
import jax.numpy as jnp
from jax import lax
from jax.experimental import pallas as pl
from jax.experimental.pallas import tpu as pltpu

F32 = jnp.float32
BF16 = jnp.bfloat16

RMS_EPS = 1e-6
TOP_K = 6
ROUTED_SCALE = 2.5
N_MOD = 6

LANES = 128
VMEM_LIMIT = 56 << 20
MOE_BLOCK = 256
FINAL_TILE = 128
RING = 3
ATT_Q_TILE = 1024
ATT_K_TILE = 256
SSM_CHUNK = 512
IDX_LANES = 128
SAMPLE_PAGES_PER_STEP = 4
SAMPLE_Q_ROWS = 8


def _cparams(*sem):
    return pltpu.CompilerParams(dimension_semantics=sem, vmem_limit_bytes=VMEM_LIMIT)


def _const_spec(shape):
    nd = len(shape)
    return pl.BlockSpec(shape, lambda *_: (0,) * nd, pipeline_mode=pl.Buffered(1))


def _rms(x, g):
    inv = lax.rsqrt(jnp.mean(x * x, axis=-1, keepdims=True) + RMS_EPS)
    return x * inv * g


LOG2E = math.log2(math.e)


def _softplus2(z2):
    return jnp.maximum(z2, 0.0) + jnp.log2(1.0 + jnp.exp2(-jnp.abs(z2)))


def _split_bf16(x):
    hi = x.astype(BF16)
    lo = (x - hi.astype(F32)).astype(BF16)
    return hi, lo


def _dot(a, b):
    return jnp.dot(a, b, preferred_element_type=F32)


def _dot_nt(a, b):
    return lax.dot_general(a, b, (((1,), (1,)), ((), ())), preferred_element_type=F32)


def _suffix_matrix(n):
    j = lax.broadcasted_iota(jnp.int32, (n, n), 0)
    s = lax.broadcasted_iota(jnp.int32, (n, n), 1)
    return (j > s).astype(BF16)


def _mod_kernel(c_ref, w_ref, b_ref, o_ref):
    c = c_ref[...]
    a = (c * jax.nn.sigmoid(c)).astype(BF16)
    o_ref[...] = _dot(a, w_ref[...].astype(BF16)) + b_ref[...]


def _modulation(c, w_ada, b_ada):
    rows, d = c.shape
    n = w_ada.shape[1]
    tn = 1024
    return pl.pallas_call(
        _mod_kernel,
        grid=(n // tn,),
        in_specs=[_const_spec((rows, d)),
                  pl.BlockSpec((d, tn), lambda j: (0, j)),
                  pl.BlockSpec((1, tn), lambda j: (0, j))],
        out_specs=pl.BlockSpec((rows, tn), lambda j: (0, j)),
        out_shape=jax.ShapeDtypeStruct((rows, n), F32),
        compiler_params=_cparams("parallel"),
        name="mod",
    )(c, w_ada, b_ada.reshape(1, n))


def _inproj_kernel(x_ref, sh_ref, sc_ref, g_ref, w_ref, q_ref, k_ref, v_ref, kb_ref, vb_ref, u_ref, *, w_att, scale,
                   kv_transposed):
    x = x_ref[...]
    h = (_rms(x, g_ref[...]) * (1.0 + sc_ref[0]) + sh_ref[0]).astype(BF16)
    q = _dot(h, w_ref[:, 0:w_att])
    q_ref[...] = (q * scale).astype(BF16)
    k = _dot(h, w_ref[:, w_att:2 * w_att])
    kb_ref[...] = k.astype(BF16)
    v = _dot(h, w_ref[:, 2 * w_att:3 * w_att])
    vb_ref[...] = v.astype(BF16)
    if kv_transposed:
        k_ref[0] = k.T
        v_ref[0] = v.T
    else:
        k_ref[...] = k
        v_ref[...] = v
    u_ref[...] = _dot(h, w_ref[:, 3 * w_att:])


def _in_proj(x2, sh, sc, g, w_bf, *, w_att, head_dim, tm, tiles_per_group, time_major_batches):
    t, d = x2.shape
    n = w_bf.shape[1]
    w_ssm = n - 3 * w_att
    mrows = sh.shape[1]
    row = lambda i: (i, 0)
    grp = lambda i: (i // tiles_per_group, 0, 0)
    if time_major_batches:
        seq = t // time_major_batches
        u_shape = (seq, time_major_batches * w_ssm)
        u_spec = pl.BlockSpec((tm, w_ssm), lambda i: (i % tiles_per_group, i // tiles_per_group))
        kv_shape = (time_major_batches, w_att, seq)
        kv_spec = pl.BlockSpec((1, w_att, tm), lambda i: (i // tiles_per_group, 0, i % tiles_per_group))
    else:
        u_shape = (t, w_ssm)
        u_spec = pl.BlockSpec((tm, w_ssm), row)
        kv_shape = (t, w_att)
        kv_spec = pl.BlockSpec((tm, w_att), row)
    return pl.pallas_call(
        functools.partial(_inproj_kernel, w_att=w_att, scale=head_dim ** -0.5 * LOG2E,
                          kv_transposed=bool(time_major_batches)),
        grid=(t // tm,),
        in_specs=[pl.BlockSpec((tm, d), row),
                  pl.BlockSpec((1, mrows, d), grp),
                  pl.BlockSpec((1, mrows, d), grp),
                  _const_spec((1, d)),
                  _const_spec((d, n))],
        out_specs=[pl.BlockSpec((tm, w_att), row), kv_spec, kv_spec, pl.BlockSpec((tm, w_att), row),
                   pl.BlockSpec((tm, w_att), row), u_spec],
        out_shape=[jax.ShapeDtypeStruct((t, w_att), BF16),
                   jax.ShapeDtypeStruct(kv_shape, F32),
                   jax.ShapeDtypeStruct(kv_shape, F32),
                   jax.ShapeDtypeStruct((t, w_att), BF16),
                   jax.ShapeDtypeStruct((t, w_att), BF16),
                   jax.ShapeDtypeStruct(u_shape, F32)],
        compiler_params=_cparams("parallel"),
        name="in_proj",
    )(x2, sh, sc, g, w_bf)


def _sb_tile(qh, ks, vs, bias, carry, acc, umat, causal):
    z = _dot_nt(qh, ks) + bias
    sp = _softplus2(z)
    if causal is not None:
        sp = jnp.where(causal, sp, 0.0)
    hi, lo = _split_bf16(sp)
    suffix = _dot(hi, umat) + _dot(lo, umat)
    a = jnp.exp2(z - sp - suffix - carry)
    if causal is not None:
        a = jnp.where(causal, a, 0.0)
    acc = acc + _dot(a.astype(BF16), vs)
    carry = carry + suffix[:, 0:1] + sp[:, 0:1]
    return carry, acc


def _attn_kernel(bias_ref, q_ref, k_ref, v_ref, o_ref, *, tq, tk, head_dim):
    hp = pl.program_id(1)
    qi = pl.program_id(2)
    q = q_ref[0]
    heads_per_block = LANES // head_dim
    ratio = tq // tk
    lane_head = lax.broadcasted_iota(jnp.int32, (1, LANES), 1) // head_dim
    umat = _suffix_matrix(tk)
    t_idx = lax.broadcasted_iota(jnp.int32, (tq, tk), 0)
    s_idx = lax.broadcasted_iota(jnp.int32, (tq, tk), 1)
    owns = [lane_head == hh for hh in range(heads_per_block)]
    qhs = [jnp.where(own, q, jnp.zeros_like(q)) for own in owns]
    biases = [bias_ref[0, hp * heads_per_block + hh] for hh in range(heads_per_block)]

    def update(kt, state, causal, row0=0):
        start = pl.multiple_of(kt * tk, tk)
        ks = k_ref[0, pl.ds(start, tk), :]
        vs = v_ref[0, pl.ds(start, tk), :]
        new = []
        for hh in range(heads_per_block):
            carry, acc = state[hh]
            if row0:
                c2, a2 = _sb_tile(qhs[hh][row0:], ks, vs, biases[hh], carry[row0:], acc[row0:], umat, causal[row0:])
                new.append((jnp.concatenate([carry[:row0], c2], axis=0), jnp.concatenate([acc[:row0], a2], axis=0)))
            else:
                new.append(_sb_tile(qhs[hh], ks, vs, biases[hh], carry, acc, umat, causal))
        return tuple(new)

    state = tuple((jnp.zeros((tq, 1), F32), jnp.zeros((tq, LANES), F32)) for _ in range(heads_per_block))
    for r in reversed(range(ratio)):
        state = update(qi * ratio + r, state, (s_idx + r * tk) < t_idx, row0=r * tk)
    state = lax.fori_loop(0, qi * ratio, lambda j, st: update(qi * ratio - 1 - j, st, None), state)
    out = state[0][1]
    for hh in range(1, heads_per_block):
        out = jnp.where(owns[hh], state[hh][1], out)
    o_ref[0] = out.astype(o_ref.dtype)


def _prompt_attention(q, k, v, sb_bias, *, head_dim):
    b, l, w = q.shape
    tq = min(ATT_Q_TILE, l)
    tk = min(ATT_K_TILE, l)
    qspec = pl.BlockSpec((1, tq, LANES), lambda bi, hp, qi: (bi, qi, hp))
    kvspec = pl.BlockSpec((1, l, LANES), lambda bi, hp, qi: (bi, 0, hp))
    return pl.pallas_call(
        functools.partial(_attn_kernel, tq=tq, tk=tk, head_dim=head_dim),
        grid=(b, w // LANES, l // tq),
        in_specs=[pl.BlockSpec(memory_space=pltpu.SMEM), qspec, kvspec, kvspec],
        out_specs=qspec,
        out_shape=jax.ShapeDtypeStruct((b, l, w), BF16),
        compiler_params=_cparams("parallel", "parallel", "arbitrary"),
        name="attn_prompt",
    )(sb_bias.reshape(1, -1).astype(F32) * LOG2E, q, k, v)


def _sb_pages(qbd, kts, vts, bias, carry, acc, umat, valid):
    page = kts[0].shape[1]
    z = _dot(qbd, jnp.concatenate(kts, axis=1)) + bias
    sp = _softplus2(z)
    if valid is not None:
        sp = jnp.where(valid, sp, 0.0)
    probs = [None] * len(kts)
    for p in reversed(range(len(kts))):
        sl = slice(p * page, (p + 1) * page)
        sp_p = sp[:, sl]
        hi, lo = _split_bf16(sp_p)
        suffix = _dot(hi, umat) + _dot(lo, umat)
        probs[p] = jnp.exp2(z[:, sl] - sp_p - suffix - carry)
        carry = carry + jnp.sum(sp_p, axis=1, keepdims=True)
    a = jnp.concatenate(probs, axis=1)
    if valid is not None:
        a = jnp.where(valid, a, 0.0)
    acc = acc + _dot_nt(jnp.concatenate(vts, axis=1), a.astype(BF16))
    return carry, acc


def _sattn_kernel(pt_ref, q_ref, kn_ref, vn_ref, bias_ref, *refs, pages_per_step, q_rows, n_new):
    del pt_ref
    k_refs = refs[:pages_per_step]
    v_refs = refs[pages_per_step:2 * pages_per_step]
    o_ref, carry_ref, acc_ref = refs[2 * pages_per_step:]
    j = pl.program_id(1)
    qbd = q_ref[0]
    bias = bias_ref[...]
    hr = qbd.shape[0]
    page = kn_ref.shape[2]
    umat = _suffix_matrix(page)

    @pl.when(j == 0)
    def _():
        i_idx = lax.broadcasted_iota(jnp.int32, (hr, page), 0) % q_rows
        s_idx = lax.broadcasted_iota(jnp.int32, (hr, page), 1)
        valid = (s_idx < i_idx) & (s_idx < n_new)
        carry, acc = _sb_pages(qbd, [kn_ref[0]], [vn_ref[0]], bias, jnp.zeros(carry_ref.shape, F32),
                               jnp.zeros(acc_ref.shape, F32), umat, valid)
        carry_ref[...] = carry
        acc_ref[...] = acc

    def flat(ref):
        x = ref[0]
        return x.reshape(x.shape[0] * x.shape[1], x.shape[2]).astype(BF16)

    carry, acc = _sb_pages(qbd, [flat(r) for r in k_refs], [flat(r) for r in v_refs], bias,
                           carry_ref[...], acc_ref[...], umat, None)
    carry_ref[...] = carry
    acc_ref[...] = acc

    @pl.when(j == pl.num_programs(1) - 1)
    def _():
        o_ref[0] = acc


def _sample_attention(q, k_new, v_new, cache_k, cache_v, page_table, sb_bias, *, batch):
    n_pool, page, h, dh = cache_k.shape
    hd = h * dh
    n_new = q.shape[0] // batch
    n_pages = page_table.shape[1]
    pps = math.gcd(SAMPLE_PAGES_PER_STEP, n_pages)
    hr = h * SAMPLE_Q_ROWS
    q4 = jnp.pad(q.reshape(batch, n_new, h, dh), ((0, 0), (0, SAMPLE_Q_ROWS - n_new), (0, 0), (0, 0)))
    qbd = jnp.einsum("bihd,gh->bgihd", q4, jnp.eye(h, dtype=q.dtype)).reshape(batch, hr, hd)

    def new_t(a):
        a = a.reshape(batch, n_new, hd).transpose(0, 2, 1)
        return jnp.pad(a, ((0, 0), (0, 0), (0, page - n_new)))

    kc = cache_k.transpose(0, 2, 3, 1)
    vc = cache_v.transpose(0, 2, 3, 1)
    pt = page_table.reshape(-1).astype(jnp.int32)
    bias = jnp.repeat(sb_bias.astype(F32) * LOG2E, SAMPLE_Q_ROWS).reshape(hr, 1)

    def paged(p):
        return pl.BlockSpec((1, h, dh, page),
                            lambda bi, j, pt_ref: (pt_ref[bi * n_pages + n_pages - pps * (j + 1) + p], 0, 0, 0))

    new_b = pl.BlockSpec((1, hd, page), lambda bi, j, pt_ref: (bi, 0, 0))
    out = pl.pallas_call(
        functools.partial(_sattn_kernel, pages_per_step=pps, q_rows=SAMPLE_Q_ROWS, n_new=n_new),
        grid_spec=pltpu.PrefetchScalarGridSpec(
            num_scalar_prefetch=1,
            grid=(batch, n_pages // pps),
            in_specs=[pl.BlockSpec((1, hr, hd), lambda bi, j, pt_ref: (bi, 0, 0)), new_b, new_b,
                      pl.BlockSpec((hr, 1), lambda bi, j, pt_ref: (0, 0))]
                     + [paged(p) for p in range(pps)] * 2,
            out_specs=pl.BlockSpec((1, hd, hr), lambda bi, j, pt_ref: (bi, 0, 0)),
            scratch_shapes=[pltpu.VMEM((hr, 1), F32), pltpu.VMEM((hd, hr), F32)]),
        out_shape=jax.ShapeDtypeStruct((batch, hd, hr), F32),
        compiler_params=_cparams("parallel", "arbitrary"),
        name="attn_sample",
    )(pt, qbd, new_t(k_new), new_t(v_new), bias, *([kc] * pps), *([vc] * pps))
    o5 = out.reshape(batch, h, dh, h, SAMPLE_Q_ROWS)
    diag = jnp.einsum("bhdhi->bihd", o5)
    return diag[:, :n_new].reshape(batch * n_new, hd)


def _ssm_kernel(u_ref, h0r_ref, h0i_ref, ar_ref, ai_ref, bre_ref, bim_ref, brel_ref, biml_ref,
                cre_ref, cim_ref, d_ref, y_ref, hr_ref, hi_ref, xr_ref, xi_ref, sr_ref, si_ref, *,
                batch, steps, precise):
    step = pl.program_id(0)
    n_chunks = bre_ref.shape[0]
    cw = bre_ref.shape[1]
    sw = bre_ref.shape[2]
    n_state = n_chunks * sw

    @pl.when(step == 0)
    def _():
        sr_ref[...] = h0r_ref[...]
        si_ref[...] = h0i_ref[...]

    u = u_ref[...]
    u_hi, u_lo = _split_bf16(u)
    for c in range(n_chunks):
        uc = u_hi[:, c * cw:(c + 1) * cw]
        br = _dot(uc, bre_ref[c])
        bi = _dot(uc, bim_ref[c])
        if precise:
            ul = u_lo[:, c * cw:(c + 1) * cw]
            br = br + _dot(ul, bre_ref[c]) + _dot(uc, brel_ref[c])
            bi = bi + _dot(ul, bim_ref[c]) + _dot(uc, biml_ref[c])
        xr_ref[:, c * sw:(c + 1) * sw] = br
        xi_ref[:, c * sw:(c + 1) * sw] = bi

    lanes = min(SSM_CHUNK, n_state)
    for c in range(n_state // lanes):
        sl = pl.ds(c * lanes, lanes)
        a_r = jnp.broadcast_to(ar_ref[:, sl], (batch, lanes))
        a_i = jnp.broadcast_to(ai_ref[:, sl], (batch, lanes))

        def body(t, state):
            x_r, x_i = state
            rows = pl.ds(pl.multiple_of(t * batch, batch), batch)
            n_r = a_r * x_r - a_i * x_i + xr_ref[rows, sl]
            n_i = a_r * x_i + a_i * x_r + xi_ref[rows, sl]
            xr_ref[rows, sl] = n_r
            xi_ref[rows, sl] = n_i
            return n_r, n_i

        x_r, x_i = lax.fori_loop(0, steps, body, (sr_ref[:, sl], si_ref[:, sl]))
        sr_ref[:, sl] = x_r
        si_ref[:, sl] = x_i

    for c in range(n_chunks):
        x_r = xr_ref[:, c * sw:(c + 1) * sw].astype(BF16)
        x_i = xi_ref[:, c * sw:(c + 1) * sw].astype(BF16)
        ch = slice(c * cw, (c + 1) * cw)
        y_ref[:, ch] = _dot(x_r, cre_ref[c]) - _dot(x_i, cim_ref[c]) + d_ref[:, ch] * u[:, ch]

    @pl.when(step == pl.num_programs(0) - 1)
    def _():
        hr_ref[...] = sr_ref[...]
        hi_ref[...] = si_ref[...]


def _ssm_params(a_re, a_im, log_dt, b_re, b_im, c_re, c_im):
    g, n = a_re.shape
    hg = b_re.shape[2]
    dt = jnp.exp(log_dt.astype(F32))[:, None]
    mag = jnp.exp(dt * a_re)
    abar_re = mag * jnp.cos(dt * a_im)
    abar_im = mag * jnp.sin(dt * a_im)
    den = a_re * a_re + a_im * a_im
    nr = abar_re - 1.0
    ni = abar_im
    coef_re = (nr * a_re + ni * a_im) / den
    coef_im = (ni * a_re - nr * a_im) / den
    bbar_re = coef_re[..., None] * b_re - coef_im[..., None] * b_im
    bbar_im = coef_re[..., None] * b_im + coef_im[..., None] * b_re
    gpc = LANES // hg
    nc = g // gpc
    eye = jnp.eye(gpc, dtype=F32)

    def pack_b(bb):
        blk = bb.reshape(nc, gpc, n, hg)
        return jnp.einsum("kgnc,gh->kgchn", blk, eye).reshape(nc, gpc * hg, gpc * n)

    def pack_c(cc):
        blk = cc.reshape(nc, gpc, hg, n)
        return jnp.einsum("kgcn,gh->kgnhc", blk, eye).reshape(nc, gpc * n, gpc * hg)

    pb_re, pb_im = pack_b(bbar_re), pack_b(bbar_im)
    bre_hi, bre_lo = _split_bf16(pb_re)
    bim_hi, bim_lo = _split_bf16(pb_im)
    return dict(ar=abar_re.reshape(1, g * n), ai=abar_im.reshape(1, g * n),
                bre=bre_hi, bim=bim_hi, brel=bre_lo, biml=bim_lo,
                cre=pack_c(c_re.astype(F32)).astype(BF16), cim=pack_c(c_im.astype(F32)).astype(BF16))


def _ssm(u_tb, h0_re, h0_im, sp, d_skip, *, batch, steps, precise):
    rows, w = u_tb.shape
    n_state = sp["ar"].shape[1]
    blk = steps * batch
    full = lambda a: _const_spec(a.shape)
    args = (sp["ar"], sp["ai"], sp["bre"], sp["bim"], sp["brel"], sp["biml"], sp["cre"], sp["cim"],
            d_skip.reshape(1, w).astype(F32))
    return pl.pallas_call(
        functools.partial(_ssm_kernel, batch=batch, steps=steps, precise=precise),
        grid=(rows // blk,),
        in_specs=[pl.BlockSpec((blk, w), lambda i: (i, 0)), full(h0_re), full(h0_im)] + [full(a) for a in args],
        out_specs=[pl.BlockSpec((blk, w), lambda i: (i, 0)),
                   pl.BlockSpec((batch, n_state), lambda i: (0, 0)),
                   pl.BlockSpec((batch, n_state), lambda i: (0, 0))],
        out_shape=[jax.ShapeDtypeStruct((rows, w), F32),
                   jax.ShapeDtypeStruct((batch, n_state), F32),
                   jax.ShapeDtypeStruct((batch, n_state), F32)],
        scratch_shapes=[pltpu.VMEM((blk, n_state), F32), pltpu.VMEM((blk, n_state), F32),
                        pltpu.VMEM((batch, n_state), F32), pltpu.VMEM((batch, n_state), F32)],
        compiler_params=_cparams("arbitrary"),
        name="ssm",
    )(u_tb, h0_re, h0_im, *args)


def _post_kernel(*refs, n_experts, aliased):
    if aliased:
        refs = refs[1:]
    (x_ref, y_ref, o_ref, g1_ref, sh_ref, sc_ref, wglu_ref, bglu_ref, gs_ref, ga_ref, wout_ref, n2_ref,
     wrh_ref, wrl_ref, rb_ref, x1_ref, h2_ref, idx_ref, gate_ref) = refs
    y = jax.nn.gelu(y_ref[...], approximate=True)
    glu = jax.nn.sigmoid(_dot(y.astype(BF16), wglu_ref[...]) + bglu_ref[...])
    n_ssm = _rms(y * glu, gs_ref[...])
    n_att = _rms(o_ref[...].astype(F32), ga_ref[...])
    merged = jnp.concatenate([n_ssm, n_att], axis=-1).astype(BF16)
    x1 = x_ref[...] + g1_ref[0] * _dot(merged, wout_ref[...])
    x1_ref[...] = x1
    h2 = _rms(x1, n2_ref[...]) * (1.0 + sc_ref[0]) + sh_ref[0]
    h2_ref[...] = h2

    h_hi, h_lo = _split_bf16(h2)
    logits = _dot(h_hi, wrh_ref[...]) + _dot(h_lo, wrh_ref[...]) + _dot(h_hi, wrl_ref[...])
    scores = jax.nn.sigmoid(logits)
    ranked = scores + rb_ref[...]
    tm = ranked.shape[0]
    e_lane = lax.broadcasted_iota(jnp.int32, (tm, n_experts), 1).astype(F32)
    o_lane = lax.broadcasted_iota(jnp.int32, (tm, IDX_LANES), 1)
    idx_out = jnp.zeros((tm, IDX_LANES), F32)
    sel_out = jnp.zeros((tm, IDX_LANES), F32)
    for k in range(TOP_K):
        best = jnp.max(ranked, axis=-1, keepdims=True)
        pick = jnp.min(jnp.where(ranked == best, e_lane, float(n_experts)), axis=-1, keepdims=True)
        chosen = e_lane == pick
        val = jnp.sum(jnp.where(chosen, scores, 0.0), axis=-1, keepdims=True)
        idx_out = jnp.where(o_lane == k, pick, idx_out)
        sel_out = jnp.where(o_lane == k, val, sel_out)
        ranked = jnp.where(chosen, -jnp.inf, ranked)
    idx_ref[...] = idx_out.astype(jnp.int32)
    gate_ref[...] = sel_out / jnp.sum(sel_out, axis=-1, keepdims=True) * ROUTED_SCALE


def _post(x2, y_src, y_spec, o_att, g1, sh2, sc2, pw, h2_prev, *, tm, tiles_per_group, h2_rows, row_offset):
    t, d = x2.shape
    w_att = o_att.shape[1]
    mrows = g1.shape[1]
    n_experts = pw["wr_hi"].shape[1]
    row = lambda i: (i, 0)
    grp = lambda i: (i // tiles_per_group, 0, 0)
    off = row_offset // tm
    weights = (pw["w_glu"], pw["b_glu"], pw["ssm_out_g"], pw["attn_out_g"], pw["w_out"], pw["norm2_g"],
               pw["wr_hi"], pw["wr_lo"], pw["router_bias"])
    in_specs = [pl.BlockSpec((tm, d), row), y_spec, pl.BlockSpec((tm, w_att), row),
                pl.BlockSpec((1, mrows, d), grp), pl.BlockSpec((1, mrows, d), grp), pl.BlockSpec((1, mrows, d), grp)]
    in_specs += [_const_spec(a.shape) for a in weights]
    args = (x2, y_src, o_att, g1, sh2, sc2) + weights
    aliases = {}
    aliased = h2_prev is not None
    if aliased:
        in_specs = [pl.BlockSpec(memory_space=pl.ANY)] + in_specs
        args = (h2_prev,) + args
        aliases = {0: 1}
    h2_shape = (h2_rows, d)
    return pl.pallas_call(
        functools.partial(_post_kernel, n_experts=n_experts, aliased=aliased),
        grid=(t // tm,),
        in_specs=in_specs,
        out_specs=[pl.BlockSpec((tm, d), row),
                   pl.BlockSpec((tm, d), lambda i: (i + off, 0)),
                   pl.BlockSpec((tm, IDX_LANES), row),
                   pl.BlockSpec((tm, IDX_LANES), row)],
        out_shape=[jax.ShapeDtypeStruct((t, d), F32),
                   jax.ShapeDtypeStruct(h2_shape, F32),
                   jax.ShapeDtypeStruct((t, IDX_LANES), jnp.int32),
                   jax.ShapeDtypeStruct((t, IDX_LANES), F32)],
        input_output_aliases=aliases,
        compiler_params=_cparams("parallel"),
        name="post",
    )(*args)


def _row_copy(src_hbm, dst, sem, row, r):
    return pltpu.make_async_copy(src_hbm.at[pl.ds(row, 1), :], dst.at[pl.ds(r, 1), :], sem)


def _gather_rows(idx_ref, src_hbm, dst, sem, n):
    for r in range(n):
        _row_copy(src_hbm, dst, sem, idx_ref[0, 0, r], r).start()


def _wait_rows(src_hbm, dst, sem):
    pltpu.make_async_copy(src_hbm.at[pl.ds(0, dst.shape[0]), :], dst, sem).wait()


def _ring_steps(i, active, last, idx_refs, src_hbm, bufs, sem, n_rows, compute):
    @pl.when(i == 0)
    def _():
        for j in range(RING - 1):
            _gather_rows(idx_refs[j], src_hbm, bufs[j], sem.at[j], n_rows)

    def step(cur):
        ahead = (cur + RING - 1) % RING
        _wait_rows(src_hbm, bufs[cur], sem.at[cur])
        _gather_rows(idx_refs[RING - 1], src_hbm, bufs[ahead], sem.at[ahead], n_rows)
        compute(bufs[cur])

        @pl.when(last)
        def _():
            for j in range(1, RING):
                _wait_rows(src_hbm, bufs[(cur + j) % RING], sem.at[(cur + j) % RING])

    for cur in range(RING):
        pl.when(jnp.logical_and(active, i % RING == cur))(functools.partial(step, cur))


def _moe_kernel(be_ref, nused_ref, *refs, blk):
    del be_ref
    idx_refs = refs[:RING]
    h_hbm, wg_ref, wu_ref, wd_ref, y_ref = refs[RING:RING + 5]
    bufs = refs[RING + 5:2 * RING + 5]
    sem = refs[2 * RING + 5]
    i = pl.program_id(0)
    n_used = nused_ref[0]

    def compute(xbuf):
        x = xbuf[...].astype(BF16)
        gate = _dot(x, wg_ref[0])
        up = _dot(x, wu_ref[0])
        act = (gate * jax.nn.sigmoid(gate) * up).astype(BF16)
        y_ref[...] = _dot(act, wd_ref[0])

    _ring_steps(i, i < n_used, i == n_used - 1, idx_refs, h_hbm, bufs, sem, blk, compute)

    @pl.when(i >= n_used)
    def _():
        y_ref[...] = jnp.zeros(y_ref.shape, y_ref.dtype)


def _moe(h_all, rows_tok, block_expert, n_used, wg, wu, wd):
    n_blocks = block_expert.shape[0]
    blk = MOE_BLOCK
    e, d, f = wg.shape
    idx3 = rows_tok.reshape(n_blocks, 1, blk)
    smem_blk = lambda im: pl.BlockSpec((1, 1, blk), im, memory_space=pltpu.SMEM)
    return pl.pallas_call(
        functools.partial(_moe_kernel, blk=blk),
        grid_spec=pltpu.PrefetchScalarGridSpec(
            num_scalar_prefetch=2,
            grid=(n_blocks,),
            in_specs=[smem_blk(functools.partial(lambda j, i, be, nu: (jnp.minimum(i + j, n_blocks - 1), 0, 0), j))
                      for j in range(RING)]
                     + [pl.BlockSpec(memory_space=pl.ANY),
                        pl.BlockSpec((1, d, f), lambda i, be, nu: (be[i], 0, 0)),
                        pl.BlockSpec((1, d, f), lambda i, be, nu: (be[i], 0, 0)),
                        pl.BlockSpec((1, f, d), lambda i, be, nu: (be[i], 0, 0))],
            out_specs=pl.BlockSpec((blk, d), lambda i, be, nu: (i, 0)),
            scratch_shapes=[pltpu.VMEM((blk, d), F32)] * RING + [pltpu.SemaphoreType.DMA((RING,))]),
        out_shape=jax.ShapeDtypeStruct((n_blocks * blk, d), F32),
        compiler_params=_cparams("arbitrary"),
        name="moe",
    )(block_expert, n_used, *([idx3] * RING), h_all, wg, wu, wd)


def _final_kernel(*refs):
    pos_refs = refs[:RING]
    y_hbm, x1_ref, h2_ref, gate_ref, g2_ref, wsg_ref, wsu_ref, wsd_ref, nf_ref, o_ref = refs[RING:RING + 10]
    bufs = refs[RING + 10:2 * RING + 10]
    sem = refs[2 * RING + 10]
    i = pl.program_id(0)
    n = pl.num_programs(0)
    tm = x1_ref.shape[0]

    def compute(ybuf):
        gates = gate_ref[...]
        routed = gates[:, 0:1] * ybuf[pl.ds(0, tm), :]
        for k in range(1, TOP_K):
            routed = routed + gates[:, k:k + 1] * ybuf[pl.ds(k * tm, tm), :]
        h = h2_ref[...].astype(BF16)
        gate = _dot(h, wsg_ref[...])
        up = _dot(h, wsu_ref[...])
        shared = _dot((gate * jax.nn.sigmoid(gate) * up).astype(BF16), wsd_ref[...])
        x = x1_ref[...] + g2_ref[0] * (routed + shared)
        o_ref[...] = _rms(x, nf_ref[...])

    _ring_steps(i, i >= 0, i == n - 1, pos_refs, y_hbm, bufs, sem, TOP_K * tm, compute)


def _final(y_rows, pos, x1, h2_all, gates, g2, fw, *, tm, tiles_per_group, row_offset):
    t, d = x1.shape
    n_tiles = t // tm
    mrows = g2.shape[1]
    pos3 = pos.reshape(n_tiles, tm, TOP_K).transpose(0, 2, 1).reshape(n_tiles, 1, TOP_K * tm)
    smem_blk = lambda im: pl.BlockSpec((1, 1, TOP_K * tm), im, memory_space=pltpu.SMEM)
    row = lambda i: (i, 0)
    off = row_offset // tm
    weights = (fw["ws_gate"], fw["ws_up"], fw["ws_down"], fw["normf_g"])
    return pl.pallas_call(
        _final_kernel,
        grid=(n_tiles,),
        in_specs=[smem_blk(functools.partial(lambda j, i: (jnp.minimum(i + j, n_tiles - 1), 0, 0), j))
                  for j in range(RING)]
                 + [pl.BlockSpec(memory_space=pl.ANY),
                  pl.BlockSpec((tm, d), row),
                  pl.BlockSpec((tm, d), lambda i: (i + off, 0)),
                  pl.BlockSpec((tm, IDX_LANES), row),
                  pl.BlockSpec((1, mrows, d), lambda i: (i // tiles_per_group, 0, 0))]
                 + [_const_spec(a.shape) for a in weights],
        out_specs=pl.BlockSpec((tm, d), row),
        out_shape=jax.ShapeDtypeStruct((t, d), F32),
        scratch_shapes=[pltpu.VMEM((TOP_K * tm, d), F32)] * RING + [pltpu.SemaphoreType.DMA((RING,))],
        compiler_params=_cparams("arbitrary"),
        name="final",
    )(*([pos3] * RING), y_rows, x1, h2_all, gates, g2, *weights)


def _dispatch(idx, n_experts):
    t, k = idx.shape
    blk = MOE_BLOCK
    n_assign = t * k
    n_blocks = -(-n_assign // blk) + n_experts
    onehot = jnp.sum((idx[:, :, None] == jnp.arange(n_experts, dtype=jnp.int32)).astype(jnp.int32), axis=1)
    before = jnp.cumsum(onehot, axis=0) - onehot
    counts = jnp.sum(onehot, axis=0)
    padded = (counts + blk - 1) // blk * blk
    pad_end = jnp.cumsum(padded)
    pad_start = pad_end - padded
    pos = pad_start[idx] + jnp.take_along_axis(before, idx, axis=1)
    tok = jnp.broadcast_to(jnp.arange(t, dtype=jnp.int32)[:, None], (t, k))
    rows_tok = jnp.zeros((n_blocks * blk,), jnp.int32).at[pos.reshape(-1)].set(tok.reshape(-1))
    block_start = jnp.arange(n_blocks, dtype=jnp.int32) * blk
    owner = jnp.sum((pad_end[None, :] <= block_start[:, None]).astype(jnp.int32), axis=1)
    block_expert = jnp.minimum(owner, n_experts - 1).astype(jnp.int32)
    n_used = (pad_end[-1] // blk).astype(jnp.int32).reshape(1)
    return pos.astype(jnp.int32), rows_tok, block_expert, n_used


def kernel(x_prompt, x_sample, c_prompt, c_sample, cache_k, cache_v, state_ssm_re, state_ssm_im, page_table,
           norm1_g, norm2_g, w_ada, b_ada, w_in, sb_bias, ssm_a_re, ssm_a_im, ssm_log_dt, ssm_b_re, ssm_b_im,
           ssm_c_re, ssm_c_im, ssm_d, w_glu, b_glu, ssm_out_g, attn_out_g, w_out, w_router, router_bias,
           we_gate, we_up, we_down, ws_gate, ws_up, ws_down, normf_g):
    depth = w_in.shape[0]
    assert depth == 1, "single-layer trunk"
    bp, lp, d = x_prompt.shape
    bs, ls, _ = x_sample.shape
    n_heads, head_dim = cache_k.shape[3], cache_k.shape[4]
    w_att = n_heads * head_dim
    g_ssm, n_ssm = ssm_a_re.shape[1], ssm_a_re.shape[2]
    n_experts = w_router.shape[2]
    tp, ts = bp * lp, bs * ls
    t_all = tp + ts
    tm = min(256, lp)
    assert lp % tm == 0 and ts % 8 == 0 and tp % FINAL_TILE == 0 and ts % FINAL_TILE == 0

    c_all = jnp.concatenate([c_prompt, c_sample], axis=0)
    mod = _modulation(c_all, w_ada[0], b_ada[0]).reshape(bp + bs, N_MOD, d)
    mod_p = [mod[:bp, i][:, None, :] for i in range(N_MOD)]
    mod_s = [jnp.repeat(mod[bp:, i], ls, axis=0)[None] for i in range(N_MOD)]

    w_in_bf = w_in[0].astype(BF16)
    g1n = norm1_g[0].reshape(1, d)
    sp = _ssm_params(ssm_a_re[0].astype(F32), ssm_a_im[0].astype(F32), ssm_log_dt[0], ssm_b_re[0].astype(F32),
                     ssm_b_im[0].astype(F32), ssm_c_re[0], ssm_c_im[0])
    wr_hi, wr_lo = _split_bf16(w_router[0].astype(F32))
    pw = dict(w_glu=w_glu[0].astype(BF16), b_glu=b_glu[0].reshape(1, -1), ssm_out_g=ssm_out_g[0].reshape(1, -1),
              attn_out_g=attn_out_g[0].reshape(1, -1), w_out=w_out[0].astype(BF16), norm2_g=norm2_g[0].reshape(1, d),
              wr_hi=wr_hi, wr_lo=wr_lo, router_bias=router_bias[0].reshape(1, -1).astype(F32))
    fw = dict(ws_gate=ws_gate[0].astype(BF16), ws_up=ws_up[0].astype(BF16), ws_down=ws_down[0].astype(BF16),
              normf_g=normf_g.reshape(1, d))

    xp2 = x_prompt.reshape(tp, d)
    tiles_pb = lp // tm
    q_p, k_p, v_p, kb_p, vb_p, u_p = _in_proj(xp2, mod_p[0], mod_p[1], g1n, w_in_bf, w_att=w_att, head_dim=head_dim,
                                              tm=tm, tiles_per_group=tiles_pb, time_major_batches=bp)
    o_att_p = _prompt_attention(q_p.reshape(bp, lp, w_att), kb_p.reshape(bp, lp, w_att),
                                vb_p.reshape(bp, lp, w_att), sb_bias[0], head_dim=head_dim).reshape(tp, w_att)
    w_ssm = u_p.shape[1] // bp
    zeros_state = jnp.zeros((bp, g_ssm * n_ssm), F32)
    ssm_steps = max(1, min(lp, 256 // bp))
    y_p, hr_p, hi_p = _ssm(u_p.reshape(lp * bp, w_ssm), zeros_state, zeros_state, sp, ssm_d[0],
                           batch=bp, steps=ssm_steps, precise=False)
    y_p_spec = pl.BlockSpec((tm, w_ssm), lambda i: (i % tiles_pb, i // tiles_pb))
    x1_p, h2_all, idx_p, gate_p = _post(xp2, y_p.reshape(lp, bp * w_ssm), y_p_spec, o_att_p, mod_p[2], mod_p[3],
                                        mod_p[4], pw, None, tm=tm, tiles_per_group=tiles_pb, h2_rows=t_all,
                                        row_offset=0)

    xs2 = x_sample.reshape(ts, d)
    q_s, k_s, v_s, kb_s, vb_s, u_s = _in_proj(xs2, mod_s[0], mod_s[1], g1n, w_in_bf, w_att=w_att, head_dim=head_dim,
                                              tm=ts, tiles_per_group=1, time_major_batches=0)
    assert ls <= SAMPLE_Q_ROWS
    o_att_s = _sample_attention(q_s, kb_s, vb_s, cache_k[0], cache_v[0], page_table, sb_bias[0],
                                batch=bs).astype(BF16)
    u_s_tb = u_s.reshape(bs, ls, w_ssm).transpose(1, 0, 2).reshape(ls * bs, w_ssm)
    y_s_tb, hr_s, hi_s = _ssm(u_s_tb, state_ssm_re[0].reshape(bs, -1).astype(F32),
                              state_ssm_im[0].reshape(bs, -1).astype(F32), sp, ssm_d[0],
                              batch=bs, steps=ls, precise=True)
    y_s = y_s_tb.reshape(ls, bs, w_ssm).transpose(1, 0, 2).reshape(ts, w_ssm)
    x1_s, h2_all, idx_s, gate_s = _post(xs2, y_s, pl.BlockSpec((ts, w_ssm), lambda i: (i, 0)), o_att_s, mod_s[2],
                                        mod_s[3], mod_s[4], pw, h2_all, tm=ts, tiles_per_group=1, h2_rows=t_all,
                                        row_offset=tp)

    idx_all = jnp.concatenate([idx_p[:, :TOP_K], idx_s[:, :TOP_K]], axis=0)
    pos, rows_tok, block_expert, n_used = _dispatch(idx_all, n_experts)
    y_rows = _moe(h2_all, rows_tok, block_expert, n_used,
                  we_gate[0].astype(BF16), we_up[0].astype(BF16), we_down[0].astype(BF16))
    ft = FINAL_TILE
    out_p = _final(y_rows, pos[:tp], x1_p, h2_all, gate_p, mod_p[5], fw, tm=ft, tiles_per_group=lp // ft,
                   row_offset=0)
    out_s = _final(y_rows, pos[tp:], x1_s, h2_all, gate_s, mod_s[5], fw, tm=ft, tiles_per_group=ts // ft,
                   row_offset=tp)

    kv_p = (1, bp, lp, n_heads, head_dim)
    kv_s = (1, bs, ls, n_heads, head_dim)
    st_p = (1, bp, g_ssm, n_ssm)
    st_s = (1, bs, g_ssm, n_ssm)
    def kv_prompt(a):
        return a.reshape(bp, n_heads, head_dim, lp).transpose(0, 3, 1, 2).reshape(kv_p)

    return (out_p.reshape(bp, lp, d), out_s.reshape(bs, ls, d),
            kv_prompt(k_p), kv_prompt(v_p), hr_p.reshape(st_p), hi_p.reshape(st_p),
            k_s.reshape(kv_s), v_s.reshape(kv_s), hr_s.reshape(st_s), hi_s.reshape(st_s))
```

```python
import functools
import math

import jax
import jax.numpy as jnp
from jax import lax
from jax.experimental import pallas as pl
from jax.experimental.pallas import tpu as pltpu

F32 = jnp.float32
BF16 = jnp.bfloat16

RMS_EPS = 1e-6
TOP_K = 6
ROUTED_SCALE = 2.5
N_MOD = 6

LANES = 128
VMEM_LIMIT = 56 << 20
MOE_BLOCK = 256
FINAL_TILE = 128
RING = 3
ATT_Q_TILE = 1024
ATT_K_TILE = 256
SSM_CHUNK = 512
IDX_LANES = 128
SAMPLE_PAGES_PER_STEP = 8
SAMPLE_Q_ROWS = 8


def _cparams(*sem):
    return pltpu.CompilerParams(dimension_semantics=sem, vmem_limit_bytes=VMEM_LIMIT)


def _const_spec(shape):
    nd = len(shape)
    return pl.BlockSpec(shape, lambda *_: (0,) * nd, pipeline_mode=pl.Buffered(1))


def _rms(x, g):
    inv = lax.rsqrt(jnp.mean(x * x, axis=-1, keepdims=True) + RMS_EPS)
    return x * inv * g


LOG2E = math.log2(math.e)


def _softplus2(z2):
    return jnp.maximum(z2, 0.0) + jnp.log2(1.0 + jnp.exp2(-jnp.abs(z2)))


def _split_bf16(x):
    hi = x.astype(BF16)
    lo = (x - hi.astype(F32)).astype(BF16)
    return hi, lo


def _dot(a, b):
    return jnp.dot(a, b, preferred_element_type=F32)


def _dot_nt(a, b):
    return lax.dot_general(a, b, (((1,), (1,)), ((), ())), preferred_element_type=F32)


def _pack_bf16_pairs(x):
    half = x.shape[1] // 2
    lo = lax.bitcast_convert_type(x[:, :half].astype(BF16).astype(F32), jnp.uint32) >> 16
    hi = lax.bitcast_convert_type(x[:, half:].astype(BF16).astype(F32), jnp.uint32) & jnp.uint32(0xFFFF0000)
    return lo | hi


def _unpack_bf16_pairs(p):
    lo = lax.bitcast_convert_type(p << 16, F32).astype(BF16)
    hi = lax.bitcast_convert_type(p & jnp.uint32(0xFFFF0000), F32).astype(BF16)
    return jnp.concatenate([lo, hi], axis=1)


def _suffix_matrix(n):
    j = lax.broadcasted_iota(jnp.int32, (n, n), 0)
    s = lax.broadcasted_iota(jnp.int32, (n, n), 1)
    return (j > s).astype(BF16)


def _mod_kernel(c_ref, w_ref, b_ref, o_ref):
    c = c_ref[...]
    a = (c * jax.nn.sigmoid(c)).astype(BF16)
    o_ref[...] = _dot(a, w_ref[...].astype(BF16)) + b_ref[...]


def _modulation(c, w_ada, b_ada):
    rows, d = c.shape
    n = w_ada.shape[1]
    tn = 1024
    return pl.pallas_call(
        _mod_kernel,
        grid=(n // tn,),
        in_specs=[_const_spec((rows, d)),
                  pl.BlockSpec((d, tn), lambda j: (0, j)),
                  pl.BlockSpec((1, tn), lambda j: (0, j))],
        out_specs=pl.BlockSpec((rows, tn), lambda j: (0, j)),
        out_shape=jax.ShapeDtypeStruct((rows, n), F32),
        compiler_params=_cparams("parallel"),
        name="mod",
    )(c, w_ada, b_ada.reshape(1, n))


def _inproj_kernel(x_ref, sh_ref, sc_ref, g_ref, w_ref, q_ref, k_ref, v_ref, kb_ref, vb_ref, u_ref, *, w_att, scale,
                   kv_transposed):
    x = x_ref[...]
    h = (_rms(x, g_ref[...]) * (1.0 + sc_ref[0]) + sh_ref[0]).astype(BF16)
    q = _dot(h, w_ref[:, 0:w_att])
    q_ref[...] = (q * scale).astype(BF16)
    k = _dot(h, w_ref[:, w_att:2 * w_att])
    kb_ref[...] = k.astype(BF16)
    v = _dot(h, w_ref[:, 2 * w_att:3 * w_att])
    vb_ref[...] = v.astype(BF16)
    if kv_transposed:
        k_ref[0] = k.T
        v_ref[0] = v.T
    else:
        k_ref[...] = k
        v_ref[...] = v
    u_ref[...] = _dot(h, w_ref[:, 3 * w_att:])


def _in_proj(x2, sh, sc, g, w_bf, *, w_att, head_dim, tm, tiles_per_group, time_major_batches):
    t, d = x2.shape
    n = w_bf.shape[1]
    w_ssm = n - 3 * w_att
    mrows = sh.shape[1]
    row = lambda i: (i, 0)
    grp = lambda i: (i // tiles_per_group, 0, 0)
    if time_major_batches:
        seq = t // time_major_batches
        u_shape = (seq, time_major_batches * w_ssm)
        u_spec = pl.BlockSpec((tm, w_ssm), lambda i: (i % tiles_per_group, i // tiles_per_group))
        kv_shape = (time_major_batches, w_att, seq)
        kv_spec = pl.BlockSpec((1, w_att, tm), lambda i: (i // tiles_per_group, 0, i % tiles_per_group))
    else:
        u_shape = (t, w_ssm)
        u_spec = pl.BlockSpec((tm, w_ssm), row)
        kv_shape = (t, w_att)
        kv_spec = pl.BlockSpec((tm, w_att), row)
    return pl.pallas_call(
        functools.partial(_inproj_kernel, w_att=w_att, scale=head_dim ** -0.5 * LOG2E,
                          kv_transposed=bool(time_major_batches)),
        grid=(t // tm,),
        in_specs=[pl.BlockSpec((tm, d), row),
                  pl.BlockSpec((1, mrows, d), grp),
                  pl.BlockSpec((1, mrows, d), grp),
                  _const_spec((1, d)),
                  _const_spec((d, n))],
        out_specs=[pl.BlockSpec((tm, w_att), row), kv_spec, kv_spec, pl.BlockSpec((tm, w_att), row),
                   pl.BlockSpec((tm, w_att), row), u_spec],
        out_shape=[jax.ShapeDtypeStruct((t, w_att), BF16),
                   jax.ShapeDtypeStruct(kv_shape, F32),
                   jax.ShapeDtypeStruct(kv_shape, F32),
                   jax.ShapeDtypeStruct((t, w_att), BF16),
                   jax.ShapeDtypeStruct((t, w_att), BF16),
                   jax.ShapeDtypeStruct(u_shape, F32)],
        compiler_params=_cparams("parallel"),
        name="in_proj",
    )(x2, sh, sc, g, w_bf)


def _sb_tile(qh, ks, vs, bias, carry, acc, umat, causal):
    z = _dot_nt(qh, ks) + bias
    sp = _softplus2(z)
    if causal is not None:
        sp = jnp.where(causal, sp, 0.0)
    suffix = _dot(sp.astype(BF16), umat)
    a = jnp.exp2(z - sp - suffix - carry)
    if causal is not None:
        a = jnp.where(causal, a, 0.0)
    acc = acc + _dot(a.astype(BF16), vs)
    carry = carry + suffix[:, 0:1] + sp[:, 0:1]
    return carry, acc


def _attn_kernel(bias_ref, q_ref, k_ref, v_ref, o_ref, *, tq, tk, head_dim):
    hp = pl.program_id(1)
    qi = pl.program_id(2)
    q = q_ref[0]
    heads_per_block = LANES // head_dim
    ratio = tq // tk
    lane_head = lax.broadcasted_iota(jnp.int32, (1, LANES), 1) // head_dim
    umat = _suffix_matrix(tk)
    t_idx = lax.broadcasted_iota(jnp.int32, (tq, tk), 0)
    s_idx = lax.broadcasted_iota(jnp.int32, (tq, tk), 1)
    owns = [lane_head == hh for hh in range(heads_per_block)]
    qhs = [jnp.where(own, q, jnp.zeros_like(q)) for own in owns]
    biases = [bias_ref[0, hp * heads_per_block + hh] for hh in range(heads_per_block)]

    def update(kt, state, causal, row0=0):
        start = pl.multiple_of(kt * tk, tk)
        ks = k_ref[0, pl.ds(start, tk), :]
        vs = v_ref[0, pl.ds(start, tk), :]
        new = []
        for hh in range(heads_per_block):
            carry, acc = state[hh]
            if row0:
                c2, a2 = _sb_tile(qhs[hh][row0:], ks, vs, biases[hh], carry[row0:], acc[row0:], umat, causal[row0:])
                new.append((jnp.concatenate([carry[:row0], c2], axis=0), jnp.concatenate([acc[:row0], a2], axis=0)))
            else:
                new.append(_sb_tile(qhs[hh], ks, vs, biases[hh], carry, acc, umat, causal))
        return tuple(new)

    state = tuple((jnp.zeros((tq, 1), F32), jnp.zeros((tq, LANES), F32)) for _ in range(heads_per_block))
    for r in reversed(range(ratio)):
        state = update(qi * ratio + r, state, (s_idx + r * tk) < t_idx, row0=r * tk)
    state = lax.fori_loop(0, qi * ratio, lambda j, st: update(qi * ratio - 1 - j, st, None), state)
    out = state[0][1]
    for hh in range(1, heads_per_block):
        out = jnp.where(owns[hh], state[hh][1], out)
    o_ref[0] = out.astype(o_ref.dtype)


def _prompt_attention(q, k, v, sb_bias, *, head_dim):
    b, l, w = q.shape
    tq = min(ATT_Q_TILE, l)
    tk = min(ATT_K_TILE, l)
    qspec = pl.BlockSpec((1, tq, LANES), lambda bi, hp, qi: (bi, qi, hp))
    kvspec = pl.BlockSpec((1, l, LANES), lambda bi, hp, qi: (bi, 0, hp))
    return pl.pallas_call(
        functools.partial(_attn_kernel, tq=tq, tk=tk, head_dim=head_dim),
        grid=(b, w // LANES, l // tq),
        in_specs=[pl.BlockSpec(memory_space=pltpu.SMEM), qspec, kvspec, kvspec],
        out_specs=qspec,
        out_shape=jax.ShapeDtypeStruct((b, l, w), BF16),
        compiler_params=_cparams("parallel", "parallel", "arbitrary"),
        name="attn_prompt",
    )(sb_bias.reshape(1, -1).astype(F32) * LOG2E, q, k, v)


def _sb_pages(qbd, kts, vts, bias, carry, acc, umat, valid):
    page = kts[0].shape[1]
    z = _dot(qbd, jnp.concatenate(kts, axis=1)) + bias
    sp = _softplus2(z)
    if valid is not None:
        sp = jnp.where(valid, sp, 0.0)
    probs = [None] * len(kts)
    for p in reversed(range(len(kts))):
        sl = slice(p * page, (p + 1) * page)
        sp_p = sp[:, sl]
        hi, lo = _split_bf16(sp_p)
        suffix = _dot(hi, umat) + _dot(lo, umat)
        probs[p] = jnp.exp2(z[:, sl] - sp_p - suffix - carry)
        carry = carry + jnp.sum(sp_p, axis=1, keepdims=True)
    a = jnp.concatenate(probs, axis=1)
    if valid is not None:
        a = jnp.where(valid, a, 0.0)
    acc = acc + _dot_nt(jnp.concatenate(vts, axis=1), a.astype(BF16))
    return carry, acc


def _sattn_kernel(pt_ref, q_ref, kn_ref, vn_ref, bias_ref, *refs, pages_per_step, q_rows, n_new):
    del pt_ref
    k_refs = refs[:pages_per_step]
    v_refs = refs[pages_per_step:2 * pages_per_step]
    o_ref, carry_ref, acc_ref = refs[2 * pages_per_step:]
    j = pl.program_id(1)
    qbd = q_ref[0]
    bias = bias_ref[...]
    hr = qbd.shape[0]
    page = kn_ref.shape[2]
    umat = _suffix_matrix(page)

    @pl.when(j == 0)
    def _():
        i_idx = lax.broadcasted_iota(jnp.int32, (hr, page), 0) % q_rows
        s_idx = lax.broadcasted_iota(jnp.int32, (hr, page), 1)
        valid = (s_idx < i_idx) & (s_idx < n_new)
        carry, acc = _sb_pages(qbd, [kn_ref[0]], [vn_ref[0]], bias, jnp.zeros(carry_ref.shape, F32),
                               jnp.zeros(acc_ref.shape, F32), umat, valid)
        carry_ref[...] = carry
        acc_ref[...] = acc

    def flat(ref):
        x = ref[0]
        return x.reshape(x.shape[0] * x.shape[1], x.shape[2]).astype(BF16)

    carry, acc = _sb_pages(qbd, [flat(r) for r in k_refs], [flat(r) for r in v_refs], bias,
                           carry_ref[...], acc_ref[...], umat, None)
    carry_ref[...] = carry
    acc_ref[...] = acc

    @pl.when(j == pl.num_programs(1) - 1)
    def _():
        o_ref[0] = acc


def _sample_attention(q, k_new, v_new, cache_k, cache_v, page_table, sb_bias, *, batch):
    n_pool, page, h, dh = cache_k.shape
    hd = h * dh
    n_new = q.shape[0] // batch
    n_pages = page_table.shape[1]
    pps = math.gcd(SAMPLE_PAGES_PER_STEP, n_pages)
    hr = h * SAMPLE_Q_ROWS
    q4 = jnp.pad(q.reshape(batch, n_new, h, dh), ((0, 0), (0, SAMPLE_Q_ROWS - n_new), (0, 0), (0, 0)))
    qbd = jnp.einsum("bihd,gh->bgihd", q4, jnp.eye(h, dtype=q.dtype)).reshape(batch, hr, hd)

    def new_t(a):
        a = a.reshape(batch, n_new, hd).transpose(0, 2, 1)
        return jnp.pad(a, ((0, 0), (0, 0), (0, page - n_new)))

    kc = cache_k.transpose(0, 2, 3, 1)
    vc = cache_v.transpose(0, 2, 3, 1)
    pt = page_table.reshape(-1).astype(jnp.int32)
    bias = jnp.repeat(sb_bias.astype(F32) * LOG2E, SAMPLE_Q_ROWS).reshape(hr, 1)

    def paged(p):
        return pl.BlockSpec((1, h, dh, page),
                            lambda bi, j, pt_ref: (pt_ref[bi * n_pages + n_pages - pps * (j + 1) + p], 0, 0, 0))

    new_b = pl.BlockSpec((1, hd, page), lambda bi, j, pt_ref: (bi, 0, 0))
    out = pl.pallas_call(
        functools.partial(_sattn_kernel, pages_per_step=pps, q_rows=SAMPLE_Q_ROWS, n_new=n_new),
        grid_spec=pltpu.PrefetchScalarGridSpec(
            num_scalar_prefetch=1,
            grid=(batch, n_pages // pps),
            in_specs=[pl.BlockSpec((1, hr, hd), lambda bi, j, pt_ref: (bi, 0, 0)), new_b, new_b,
                      pl.BlockSpec((hr, 1), lambda bi, j, pt_ref: (0, 0))]
                     + [paged(p) for p in range(pps)] * 2,
            out_specs=pl.BlockSpec((1, hd, hr), lambda bi, j, pt_ref: (bi, 0, 0)),
            scratch_shapes=[pltpu.VMEM((hr, 1), F32), pltpu.VMEM((hd, hr), F32)]),
        out_shape=jax.ShapeDtypeStruct((batch, hd, hr), F32),
        compiler_params=_cparams("parallel", "arbitrary"),
        name="attn_sample",
    )(pt, qbd, new_t(k_new), new_t(v_new), bias, *([kc] * pps), *([vc] * pps))
    o5 = out.reshape(batch, h, dh, h, SAMPLE_Q_ROWS)
    diag = jnp.einsum("bhdhi->bihd", o5)
    return diag[:, :n_new].reshape(batch * n_new, hd)


def _ssm_kernel(u_ref, h0r_ref, h0i_ref, ar_ref, ai_ref, bre_ref, bim_ref, brel_ref, biml_ref,
                cre_ref, cim_ref, d_ref, y_ref, hr_ref, hi_ref, xr_ref, xi_ref, sr_ref, si_ref, *,
                batch, steps, precise):
    step = pl.program_id(0)
    n_chunks = bre_ref.shape[0]
    cw = bre_ref.shape[1]
    sw = bre_ref.shape[2]
    n_state = n_chunks * sw

    @pl.when(step == 0)
    def _():
        sr_ref[...] = h0r_ref[...]
        si_ref[...] = h0i_ref[...]

    u = u_ref[...]
    u_hi, u_lo = _split_bf16(u)
    for c in range(n_chunks):
        uc = u_hi[:, c * cw:(c + 1) * cw]
        br = _dot(uc, bre_ref[c])
        bi = _dot(uc, bim_ref[c])
        if precise:
            ul = u_lo[:, c * cw:(c + 1) * cw]
            br = br + _dot(ul, bre_ref[c]) + _dot(uc, brel_ref[c])
            bi = bi + _dot(ul, bim_ref[c]) + _dot(uc, biml_ref[c])
        xr_ref[:, c * sw:(c + 1) * sw] = br
        xi_ref[:, c * sw:(c + 1) * sw] = bi

    lanes = min(SSM_CHUNK, n_state)
    for c in range(n_state // lanes):
        sl = pl.ds(c * lanes, lanes)
        a_r = jnp.broadcast_to(ar_ref[:, sl], (batch, lanes))
        a_i = jnp.broadcast_to(ai_ref[:, sl], (batch, lanes))

        def body(t, state):
            x_r, x_i = state
            rows = pl.ds(pl.multiple_of(t * batch, batch), batch)
            n_r = a_r * x_r - a_i * x_i + xr_ref[rows, sl]
            n_i = a_r * x_i + a_i * x_r + xi_ref[rows, sl]
            xr_ref[rows, sl] = n_r
            xi_ref[rows, sl] = n_i
            return n_r, n_i

        x_r, x_i = lax.fori_loop(0, steps, body, (sr_ref[:, sl], si_ref[:, sl]))
        sr_ref[:, sl] = x_r
        si_ref[:, sl] = x_i

    for c in range(n_chunks):
        x_r = xr_ref[:, c * sw:(c + 1) * sw].astype(BF16)
        x_i = xi_ref[:, c * sw:(c + 1) * sw].astype(BF16)
        ch = slice(c * cw, (c + 1) * cw)
        y_ref[:, ch] = _dot(x_r, cre_ref[c]) - _dot(x_i, cim_ref[c]) + d_ref[:, ch] * u[:, ch]

    @pl.when(step == pl.num_programs(0) - 1)
    def _():
        hr_ref[...] = sr_ref[...]
        hi_ref[...] = si_ref[...]


def _ssm_params(a_re, a_im, log_dt, b_re, b_im, c_re, c_im):
    g, n = a_re.shape
    hg = b_re.shape[2]
    dt = jnp.exp(log_dt.astype(F32))[:, None]
    mag = jnp.exp(dt * a_re)
    abar_re = mag * jnp.cos(dt * a_im)
    abar_im = mag * jnp.sin(dt * a_im)
    den = a_re * a_re + a_im * a_im
    nr = abar_re - 1.0
    ni = abar_im
    coef_re = (nr * a_re + ni * a_im) / den
    coef_im = (ni * a_re - nr * a_im) / den
    bbar_re = coef_re[..., None] * b_re - coef_im[..., None] * b_im
    bbar_im = coef_re[..., None] * b_im + coef_im[..., None] * b_re
    gpc = LANES // hg
    nc = g // gpc
    eye = jnp.eye(gpc, dtype=F32)

    def pack_b(bb):
        blk = bb.reshape(nc, gpc, n, hg)
        return jnp.einsum("kgnc,gh->kgchn", blk, eye).reshape(nc, gpc * hg, gpc * n)

    def pack_c(cc):
        blk = cc.reshape(nc, gpc, hg, n)
        return jnp.einsum("kgcn,gh->kgnhc", blk, eye).reshape(nc, gpc * n, gpc * hg)

    pb_re, pb_im = pack_b(bbar_re), pack_b(bbar_im)
    bre_hi, bre_lo = _split_bf16(pb_re)
    bim_hi, bim_lo = _split_bf16(pb_im)
    return dict(ar=abar_re.reshape(1, g * n), ai=abar_im.reshape(1, g * n),
                bre=bre_hi, bim=bim_hi, brel=bre_lo, biml=bim_lo,
                cre=pack_c(c_re.astype(F32)).astype(BF16), cim=pack_c(c_im.astype(F32)).astype(BF16))


def _ssm(u_tb, h0_re, h0_im, sp, d_skip, *, batch, steps, precise):
    rows, w = u_tb.shape
    n_state = sp["ar"].shape[1]
    blk = steps * batch
    full = lambda a: _const_spec(a.shape)
    args = (sp["ar"], sp["ai"], sp["bre"], sp["bim"], sp["brel"], sp["biml"], sp["cre"], sp["cim"],
            d_skip.reshape(1, w).astype(F32))
    return pl.pallas_call(
        functools.partial(_ssm_kernel, batch=batch, steps=steps, precise=precise),
        grid=(rows // blk,),
        in_specs=[pl.BlockSpec((blk, w), lambda i: (i, 0)), full(h0_re), full(h0_im)] + [full(a) for a in args],
        out_specs=[pl.BlockSpec((blk, w), lambda i: (i, 0)),
                   pl.BlockSpec((batch, n_state), lambda i: (0, 0)),
                   pl.BlockSpec((batch, n_state), lambda i: (0, 0))],
        out_shape=[jax.ShapeDtypeStruct((rows, w), F32),
                   jax.ShapeDtypeStruct((batch, n_state), F32),
                   jax.ShapeDtypeStruct((batch, n_state), F32)],
        scratch_shapes=[pltpu.VMEM((blk, n_state), F32), pltpu.VMEM((blk, n_state), F32),
                        pltpu.VMEM((batch, n_state), F32), pltpu.VMEM((batch, n_state), F32)],
        compiler_params=_cparams("arbitrary"),
        name="ssm",
    )(u_tb, h0_re, h0_im, *args)


def _post_kernel(*refs, n_experts, aliased):
    if aliased:
        refs = refs[1:]
    (x_ref, y_ref, o_ref, g1_ref, sh_ref, sc_ref, wglu_ref, bglu_ref, gs_ref, ga_ref, wout_ref, n2_ref,
     wrh_ref, wrl_ref, rb_ref, x1_ref, h2_ref, idx_ref, gate_ref) = refs
    y = jax.nn.gelu(y_ref[...], approximate=True)
    glu = jax.nn.sigmoid(_dot(y.astype(BF16), wglu_ref[...]) + bglu_ref[...])
    n_ssm = _rms(y * glu, gs_ref[...])
    n_att = _rms(o_ref[...].astype(F32), ga_ref[...])
    merged = jnp.concatenate([n_ssm, n_att], axis=-1).astype(BF16)
    x1 = x_ref[...] + g1_ref[0] * _dot(merged, wout_ref[...])
    x1_ref[...] = x1
    h2 = _rms(x1, n2_ref[...]) * (1.0 + sc_ref[0]) + sh_ref[0]
    h2_ref[...] = _pack_bf16_pairs(h2)

    h_hi, h_lo = _split_bf16(h2)
    logits = _dot(h_hi, wrh_ref[...]) + _dot(h_lo, wrh_ref[...]) + _dot(h_hi, wrl_ref[...])
    scores = jax.nn.sigmoid(logits)
    ranked = scores + rb_ref[...]
    tm = ranked.shape[0]
    e_lane = lax.broadcasted_iota(jnp.int32, (tm, n_experts), 1).astype(F32)
    o_lane = lax.broadcasted_iota(jnp.int32, (tm, IDX_LANES), 1)
    idx_out = jnp.zeros((tm, IDX_LANES), F32)
    sel_out = jnp.zeros((tm, IDX_LANES), F32)
    for k in range(TOP_K):
        best = jnp.max(ranked, axis=-1, keepdims=True)
        pick = jnp.min(jnp.where(ranked == best, e_lane, float(n_experts)), axis=-1, keepdims=True)
        chosen = e_lane == pick
        val = jnp.sum(jnp.where(chosen, scores, 0.0), axis=-1, keepdims=True)
        idx_out = jnp.where(o_lane == k, pick, idx_out)
        sel_out = jnp.where(o_lane == k, val, sel_out)
        ranked = jnp.where(chosen, -jnp.inf, ranked)
    idx_ref[...] = idx_out.astype(jnp.int32)
    gate_ref[...] = sel_out / jnp.sum(sel_out, axis=-1, keepdims=True) * ROUTED_SCALE


def _post(x2, y_src, y_spec, o_att, g1, sh2, sc2, pw, h2_prev, *, tm, tiles_per_group, h2_rows, row_offset):
    t, d = x2.shape
    w_att = o_att.shape[1]
    mrows = g1.shape[1]
    n_experts = pw["wr_hi"].shape[1]
    row = lambda i: (i, 0)
    grp = lambda i: (i // tiles_per_group, 0, 0)
    off = row_offset // tm
    weights = (pw["w_glu"], pw["b_glu"], pw["ssm_out_g"], pw["attn_out_g"], pw["w_out"], pw["norm2_g"],
               pw["wr_hi"], pw["wr_lo"], pw["router_bias"])
    in_specs = [pl.BlockSpec((tm, d), row), y_spec, pl.BlockSpec((tm, w_att), row),
                pl.BlockSpec((1, mrows, d), grp), pl.BlockSpec((1, mrows, d), grp), pl.BlockSpec((1, mrows, d), grp)]
    in_specs += [_const_spec(a.shape) for a in weights]
    args = (x2, y_src, o_att, g1, sh2, sc2) + weights
    aliases = {}
    aliased = h2_prev is not None
    if aliased:
        in_specs = [pl.BlockSpec(memory_space=pl.ANY)] + in_specs
        args = (h2_prev,) + args
        aliases = {0: 1}
    h2_shape = (h2_rows, d // 2)
    return pl.pallas_call(
        functools.partial(_post_kernel, n_experts=n_experts, aliased=aliased),
        grid=(t // tm,),
        in_specs=in_specs,
        out_specs=[pl.BlockSpec((tm, d), row),
                   pl.BlockSpec((tm, d // 2), lambda i: (i + off, 0)),
                   pl.BlockSpec((tm, IDX_LANES), row),
                   pl.BlockSpec((tm, IDX_LANES), row)],
        out_shape=[jax.ShapeDtypeStruct((t, d), F32),
                   jax.ShapeDtypeStruct(h2_shape, jnp.uint32),
                   jax.ShapeDtypeStruct((t, IDX_LANES), jnp.int32),
                   jax.ShapeDtypeStruct((t, IDX_LANES), F32)],
        input_output_aliases=aliases,
        compiler_params=_cparams("parallel"),
        name="post",
    )(*args)


def _row_copy(src_hbm, dst, sem, row, r):
    return pltpu.make_async_copy(src_hbm.at[pl.ds(row, 1), :], dst.at[pl.ds(r, 1), :], sem)


def _gather_rows(idx_ref, src_hbm, dst, sem, n):
    for r in range(n):
        _row_copy(src_hbm, dst, sem, idx_ref[0, 0, r], r).start()


def _wait_rows(src_hbm, dst, sem):
    pltpu.make_async_copy(src_hbm.at[pl.ds(0, dst.shape[0]), :], dst, sem).wait()


def _ring_steps(i, active, last, idx_refs, src_hbm, bufs, sem, n_rows, compute):
    @pl.when(i == 0)
    def _():
        for j in range(RING - 1):
            _gather_rows(idx_refs[j], src_hbm, bufs[j], sem.at[j], n_rows)

    def step(cur):
        ahead = (cur + RING - 1) % RING
        _wait_rows(src_hbm, bufs[cur], sem.at[cur])
        _gather_rows(idx_refs[RING - 1], src_hbm, bufs[ahead], sem.at[ahead], n_rows)
        compute(bufs[cur])

        @pl.when(last)
        def _():
            for j in range(1, RING):
                _wait_rows(src_hbm, bufs[(cur + j) % RING], sem.at[(cur + j) % RING])

    for cur in range(RING):
        pl.when(jnp.logical_and(active, i % RING == cur))(functools.partial(step, cur))


def _moe_kernel(be_ref, nused_ref, *refs, blk):
    del be_ref
    idx_refs = refs[:RING]
    h_hbm, wg_ref, wu_ref, wd_ref, y_ref = refs[RING:RING + 5]
    bufs = refs[RING + 5:2 * RING + 5]
    sem = refs[2 * RING + 5]
    i = pl.program_id(0)
    n_used = nused_ref[0]

    def compute(xbuf):
        x = _unpack_bf16_pairs(xbuf[...])
        gate = _dot(x, wg_ref[0])
        up = _dot(x, wu_ref[0])
        act = (gate * jax.nn.sigmoid(gate) * up).astype(BF16)
        y_ref[...] = _dot(act, wd_ref[0])

    _ring_steps(i, i < n_used, i == n_used - 1, idx_refs, h_hbm, bufs, sem, blk, compute)

    @pl.when(i >= n_used)
    def _():
        y_ref[...] = jnp.zeros(y_ref.shape, y_ref.dtype)


def _moe(h_all, rows_tok, block_expert, n_used, wg, wu, wd):
    n_blocks = block_expert.shape[0]
    blk = MOE_BLOCK
    e, d, f = wg.shape
    idx3 = rows_tok.reshape(n_blocks, 1, blk)
    smem_blk = lambda im: pl.BlockSpec((1, 1, blk), im, memory_space=pltpu.SMEM)
    return pl.pallas_call(
        functools.partial(_moe_kernel, blk=blk),
        grid_spec=pltpu.PrefetchScalarGridSpec(
            num_scalar_prefetch=2,
            grid=(n_blocks,),
            in_specs=[smem_blk(functools.partial(lambda j, i, be, nu: (jnp.minimum(i + j, n_blocks - 1), 0, 0), j))
                      for j in range(RING)]
                     + [pl.BlockSpec(memory_space=pl.ANY),
                        pl.BlockSpec((1, d, f), lambda i, be, nu: (be[i], 0, 0)),
                        pl.BlockSpec((1, d, f), lambda i, be, nu: (be[i], 0, 0)),
                        pl.BlockSpec((1, f, d), lambda i, be, nu: (be[i], 0, 0))],
            out_specs=pl.BlockSpec((blk, d), lambda i, be, nu: (i, 0)),
            scratch_shapes=[pltpu.VMEM((blk, d // 2), jnp.uint32)] * RING + [pltpu.SemaphoreType.DMA((RING,))]),
        out_shape=jax.ShapeDtypeStruct((n_blocks * blk, d), F32),
        compiler_params=_cparams("arbitrary"),
        name="moe",
    )(block_expert, n_used, *([idx3] * RING), h_all, wg, wu, wd)


def _final_kernel(*refs):
    pos_refs = refs[:RING]
    y_hbm, x1_ref, h2_ref, gate_ref, g2_ref, wsg_ref, wsu_ref, wsd_ref, nf_ref, o_ref = refs[RING:RING + 10]
    bufs = refs[RING + 10:2 * RING + 10]
    sem = refs[2 * RING + 10]
    i = pl.program_id(0)
    n = pl.num_programs(0)
    tm = x1_ref.shape[0]

    def compute(ybuf):
        gates = gate_ref[...]
        routed = gates[:, 0:1] * ybuf[pl.ds(0, tm), :]
        for k in range(1, TOP_K):
            routed = routed + gates[:, k:k + 1] * ybuf[pl.ds(k * tm, tm), :]
        h = _unpack_bf16_pairs(h2_ref[...])
        gate = _dot(h, wsg_ref[...])
        up = _dot(h, wsu_ref[...])
        shared = _dot((gate * jax.nn.sigmoid(gate) * up).astype(BF16), wsd_ref[...])
        x = x1_ref[...] + g2_ref[0] * (routed + shared)
        o_ref[...] = _rms(x, nf_ref[...])

    _ring_steps(i, i >= 0, i == n - 1, pos_refs, y_hbm, bufs, sem, TOP_K * tm, compute)


def _final(y_rows, pos, x1, h2_all, gates, g2, fw, *, tm, tiles_per_group, row_offset):
    t, d = x1.shape
    n_tiles = t // tm
    mrows = g2.shape[1]
    pos3 = pos.reshape(n_tiles, tm, TOP_K).transpose(0, 2, 1).reshape(n_tiles, 1, TOP_K * tm)
    smem_blk = lambda im: pl.BlockSpec((1, 1, TOP_K * tm), im, memory_space=pltpu.SMEM)
    row = lambda i: (i, 0)
    off = row_offset // tm
    weights = (fw["ws_gate"], fw["ws_up"], fw["ws_down"], fw["normf_g"])
    return pl.pallas_call(
        _final_kernel,
        grid=(n_tiles,),
        in_specs=[smem_blk(functools.partial(lambda j, i: (jnp.minimum(i + j, n_tiles - 1), 0, 0), j))
                  for j in range(RING)]
                 + [pl.BlockSpec(memory_space=pl.ANY),
                  pl.BlockSpec((tm, d), row),
                  pl.BlockSpec((tm, d // 2), lambda i: (i + off, 0)),
                  pl.BlockSpec((tm, IDX_LANES), row),
                  pl.BlockSpec((1, mrows, d), lambda i: (i // tiles_per_group, 0, 0))]
                 + [_const_spec(a.shape) for a in weights],
        out_specs=pl.BlockSpec((tm, d), row),
        out_shape=jax.ShapeDtypeStruct((t, d), F32),
        scratch_shapes=[pltpu.VMEM((TOP_K * tm, d), F32)] * RING + [pltpu.SemaphoreType.DMA((RING,))],
        compiler_params=_cparams("arbitrary"),
        name="final",
    )(*([pos3] * RING), y_rows, x1, h2_all, gates, g2, *weights)


def _dispatch(idx, n_experts):
    t, k = idx.shape
    blk = MOE_BLOCK
    n_assign = t * k
    n_blocks = -(-n_assign // blk) + n_experts
    onehot = jnp.sum((idx[:, :, None] == jnp.arange(n_experts, dtype=jnp.int32)).astype(jnp.int32), axis=1)
    before = jnp.cumsum(onehot, axis=0) - onehot
    counts = jnp.sum(onehot, axis=0)
    padded = (counts + blk - 1) // blk * blk
    pad_end = jnp.cumsum(padded)
    pad_start = pad_end - padded
    pos = pad_start[idx] + jnp.take_along_axis(before, idx, axis=1)
    tok = jnp.broadcast_to(jnp.arange(t, dtype=jnp.int32)[:, None], (t, k))
    rows_tok = jnp.zeros((n_blocks * blk,), jnp.int32).at[pos.reshape(-1)].set(tok.reshape(-1))
    block_start = jnp.arange(n_blocks, dtype=jnp.int32) * blk
    owner = jnp.sum((pad_end[None, :] <= block_start[:, None]).astype(jnp.int32), axis=1)
    block_expert = jnp.minimum(owner, n_experts - 1).astype(jnp.int32)
    n_used = (pad_end[-1] // blk).astype(jnp.int32).reshape(1)
    return pos.astype(jnp.int32), rows_tok, block_expert, n_used


def kernel(x_prompt, x_sample, c_prompt, c_sample, cache_k, cache_v, state_ssm_re, state_ssm_im, page_table,
           norm1_g, norm2_g, w_ada, b_ada, w_in, sb_bias, ssm_a_re, ssm_a_im, ssm_log_dt, ssm_b_re, ssm_b_im,
           ssm_c_re, ssm_c_im, ssm_d, w_glu, b_glu, ssm_out_g, attn_out_g, w_out, w_router, router_bias,
           we_gate, we_up, we_down, ws_gate, ws_up, ws_down, normf_g):
    depth = w_in.shape[0]
    assert depth == 1, "single-layer trunk"
    bp, lp, d = x_prompt.shape
    bs, ls, _ = x_sample.shape
    n_heads, head_dim = cache_k.shape[3], cache_k.shape[4]
    w_att = n_heads * head_dim
    g_ssm, n_ssm = ssm_a_re.shape[1], ssm_a_re.shape[2]
    n_experts = w_router.shape[2]
    tp, ts = bp * lp, bs * ls
    t_all = tp + ts
    tm = min(256, lp)
    assert lp % tm == 0 and ts % 8 == 0 and tp % FINAL_TILE == 0 and ts % FINAL_TILE == 0

    c_all = jnp.concatenate([c_prompt, c_sample], axis=0)
    mod = _modulation(c_all, w_ada[0], b_ada[0]).reshape(bp + bs, N_MOD, d)
    mod_p = [mod[:bp, i][:, None, :] for i in range(N_MOD)]
    mod_s = [jnp.repeat(mod[bp:, i], ls, axis=0)[None] for i in range(N_MOD)]

    w_in_bf = w_in[0].astype(BF16)
    g1n = norm1_g[0].reshape(1, d)
    sp = _ssm_params(ssm_a_re[0].astype(F32), ssm_a_im[0].astype(F32), ssm_log_dt[0], ssm_b_re[0].astype(F32),
                     ssm_b_im[0].astype(F32), ssm_c_re[0], ssm_c_im[0])
    wr_hi, wr_lo = _split_bf16(w_router[0].astype(F32))
    pw = dict(w_glu=w_glu[0].astype(BF16), b_glu=b_glu[0].reshape(1, -1), ssm_out_g=ssm_out_g[0].reshape(1, -1),
              attn_out_g=attn_out_g[0].reshape(1, -1), w_out=w_out[0].astype(BF16), norm2_g=norm2_g[0].reshape(1, d),
              wr_hi=wr_hi, wr_lo=wr_lo, router_bias=router_bias[0].reshape(1, -1).astype(F32))
    fw = dict(ws_gate=ws_gate[0].astype(BF16), ws_up=ws_up[0].astype(BF16), ws_down=ws_down[0].astype(BF16),
              normf_g=normf_g.reshape(1, d))

    xp2 = x_prompt.reshape(tp, d)
    tiles_pb = lp // tm
    q_p, k_p, v_p, kb_p, vb_p, u_p = _in_proj(xp2, mod_p[0], mod_p[1], g1n, w_in_bf, w_att=w_att, head_dim=head_dim,
                                              tm=tm, tiles_per_group=tiles_pb, time_major_batches=bp)
    o_att_p = _prompt_attention(q_p.reshape(bp, lp, w_att), kb_p.reshape(bp, lp, w_att),
                                vb_p.reshape(bp, lp, w_att), sb_bias[0], head_dim=head_dim).reshape(tp, w_att)
    w_ssm = u_p.shape[1] // bp
    zeros_state = jnp.zeros((bp, g_ssm * n_ssm), F32)
    ssm_steps = max(1, min(lp, 256 // bp))
    y_p, hr_p, hi_p = _ssm(u_p.reshape(lp * bp, w_ssm), zeros_state, zeros_state, sp, ssm_d[0],
                           batch=bp, steps=ssm_steps, precise=False)
    y_p_spec = pl.BlockSpec((tm, w_ssm), lambda i: (i % tiles_pb, i // tiles_pb))
    x1_p, h2_all, idx_p, gate_p = _post(xp2, y_p.reshape(lp, bp * w_ssm), y_p_spec, o_att_p, mod_p[2], mod_p[3],
                                        mod_p[4], pw, None, tm=tm, tiles_per_group=tiles_pb, h2_rows=t_all,
                                        row_offset=0)

    xs2 = x_sample.reshape(ts, d)
    q_s, k_s, v_s, kb_s, vb_s, u_s = _in_proj(xs2, mod_s[0], mod_s[1], g1n, w_in_bf, w_att=w_att, head_dim=head_dim,
                                              tm=ts, tiles_per_group=1, time_major_batches=0)
    assert ls <= SAMPLE_Q_ROWS
    o_att_s = _sample_attention(q_s, kb_s, vb_s, cache_k[0], cache_v[0], page_table, sb_bias[0],
                                batch=bs).astype(BF16)
    u_s_tb = u_s.reshape(bs, ls, w_ssm).transpose(1, 0, 2).reshape(ls * bs, w_ssm)
    y_s_tb, hr_s, hi_s = _ssm(u_s_tb, state_ssm_re[0].reshape(bs, -1).astype(F32),
                              state_ssm_im[0].reshape(bs, -1).astype(F32), sp, ssm_d[0],
                              batch=bs, steps=ls, precise=True)
    y_s = y_s_tb.reshape(ls, bs, w_ssm).transpose(1, 0, 2).reshape(ts, w_ssm)
    x1_s, h2_all, idx_s, gate_s = _post(xs2, y_s, pl.BlockSpec((ts, w_ssm), lambda i: (i, 0)), o_att_s, mod_s[2],
                                        mod_s[3], mod_s[4], pw, h2_all, tm=ts, tiles_per_group=1, h2_rows=t_all,
                                        row_offset=tp)

    idx_all = jnp.concatenate([idx_p[:, :TOP_K], idx_s[:, :TOP_K]], axis=0)
    pos, rows_tok, block_expert, n_used = _dispatch(idx_all, n_experts)
    y_rows = _moe(h2_all, rows_tok, block_expert, n_used,
                  we_gate[0].astype(BF16), we_up[0].astype(BF16), we_down[0].astype(BF16))
    ft = FINAL_TILE
    out_p = _final(y_rows, pos[:tp], x1_p, h2_all, gate_p, mod_p[5], fw, tm=ft, tiles_per_group=lp // ft,
                   row_offset=0)
    out_s = _final(y_rows, pos[tp:], x1_s, h2_all, gate_s, mod_s[5], fw, tm=ft, tiles_per_group=ts // ft,
                   row_offset=tp)

    kv_p = (1, bp, lp, n_heads, head_dim)
    kv_s = (1, bs, ls, n_heads, head_dim)
    st_p = (1, bp, g_ssm, n_ssm)
    st_s = (1, bs, g_ssm, n_ssm)
    def kv_prompt(a):
        return a.reshape(bp, n_heads, head_dim, lp).transpose(0, 3, 1, 2).reshape(kv_p)

    return (out_p.reshape(bp, lp, d), out_s.reshape(bs, ls, d),
            kv_prompt(k_p), kv_prompt(v_p), hr_p.reshape(st_p), hi_p.reshape(st_p),
            k_s.reshape(kv_s), v_s.reshape(kv_s), hr_s.reshape(st_s), hi_s.reshape(st_s))
```

```python
import functools
import math

import jax
import jax.numpy as jnp
from jax import lax
from jax.experimental import pallas as pl
from jax.experimental.pallas import tpu as pltpu

F32 = jnp.float32
BF16 = jnp.bfloat16

RMS_EPS = 1e-6
TOP_K = 6
ROUTED_SCALE = 2.5
N_MOD = 6

LANES = 128
VMEM_LIMIT = 56 << 20
MOE_BLOCK = 256
FINAL_TILE = 128
RING = 3
ATT_Q_TILE = 1024
ATT_K_TILE = 256
SSM_CHUNK = 512
IDX_LANES = 128
SAMPLE_PAGES_PER_STEP = 8
SAMPLE_Q_ROWS = 8


def _cparams(*sem):
    return pltpu.CompilerParams(dimension_semantics=sem, vmem_limit_bytes=VMEM_LIMIT)


def _const_spec(shape):
    nd = len(shape)
    return pl.BlockSpec(shape, lambda *_: (0,) * nd, pipeline_mode=pl.Buffered(1))


def _rms(x, g):
    inv = lax.rsqrt(jnp.mean(x * x, axis=-1, keepdims=True) + RMS_EPS)
    return x * inv * g


LOG2E = math.log2(math.e)


def _softplus2(z2):
    return jnp.maximum(z2, 0.0) + jnp.log2(1.0 + jnp.exp2(-jnp.abs(z2)))


def _split_bf16(x):
    hi = x.astype(BF16)
    lo = (x - hi.astype(F32)).astype(BF16)
    return hi, lo


def _dot(a, b):
    return jnp.dot(a, b, preferred_element_type=F32)


def _dot_nt(a, b):
    return lax.dot_general(a, b, (((1,), (1,)), ((), ())), preferred_element_type=F32)


def _pack_bf16_pairs(x):
    half = x.shape[1] // 2
    lo = lax.bitcast_convert_type(x[:, :half].astype(BF16).astype(F32), jnp.uint32) >> 16
    hi = lax.bitcast_convert_type(x[:, half:].astype(BF16).astype(F32), jnp.uint32) & jnp.uint32(0xFFFF0000)
    return lo | hi


def _unpack_bf16_pairs(p):
    lo = lax.bitcast_convert_type(p << 16, F32).astype(BF16)
    hi = lax.bitcast_convert_type(p & jnp.uint32(0xFFFF0000), F32).astype(BF16)
    return jnp.concatenate([lo, hi], axis=1)


def _suffix_matrix(n):
    j = lax.broadcasted_iota(jnp.int32, (n, n), 0)
    s = lax.broadcasted_iota(jnp.int32, (n, n), 1)
    return (j > s).astype(BF16)


def _mod_kernel(c_ref, w_ref, b_ref, o_ref):
    c = c_ref[...]
    a = (c * jax.nn.sigmoid(c)).astype(BF16)
    o_ref[...] = _dot(a, w_ref[...].astype(BF16)) + b_ref[...]


def _modulation(c, w_ada, b_ada):
    rows, d = c.shape
    n = w_ada.shape[1]
    tn = 1024
    return pl.pallas_call(
        _mod_kernel,
        grid=(n // tn,),
        in_specs=[_const_spec((rows, d)),
                  pl.BlockSpec((d, tn), lambda j: (0, j)),
                  pl.BlockSpec((1, tn), lambda j: (0, j))],
        out_specs=pl.BlockSpec((rows, tn), lambda j: (0, j)),
        out_shape=jax.ShapeDtypeStruct((rows, n), F32),
        compiler_params=_cparams("parallel"),
        name="mod",
    )(c, w_ada, b_ada.reshape(1, n))


def _inproj_kernel(x_ref, sh_ref, sc_ref, g_ref, w_ref, q_ref, k_ref, v_ref, kb_ref, vb_ref, u_ref, *, w_att, scale,
                   kv_transposed):
    x = x_ref[...]
    h = (_rms(x, g_ref[...]) * (1.0 + sc_ref[0]) + sh_ref[0]).astype(BF16)
    q = _dot(h, w_ref[:, 0:w_att])
    q_ref[...] = (q * scale).astype(BF16)
    k = _dot(h, w_ref[:, w_att:2 * w_att])
    kb_ref[...] = k.astype(BF16)
    v = _dot(h, w_ref[:, 2 * w_att:3 * w_att])
    vb_ref[...] = v.astype(BF16)
    if kv_transposed:
        k_ref[0] = k.T
        v_ref[0] = v.T
    else:
        k_ref[...] = k
        v_ref[...] = v
    u_ref[...] = _dot(h, w_ref[:, 3 * w_att:])


def _in_proj(x2, sh, sc, g, w_bf, *, w_att, head_dim, tm, tiles_per_group, time_major_batches):
    t, d = x2.shape
    n = w_bf.shape[1]
    w_ssm = n - 3 * w_att
    mrows = sh.shape[1]
    row = lambda i: (i, 0)
    grp = lambda i: (i // tiles_per_group, 0, 0)
    if time_major_batches:
        seq = t // time_major_batches
        u_shape = (seq, time_major_batches * w_ssm)
        u_spec = pl.BlockSpec((tm, w_ssm), lambda i: (i % tiles_per_group, i // tiles_per_group))
        kv_shape = (time_major_batches, w_att, seq)
        kv_spec = pl.BlockSpec((1, w_att, tm), lambda i: (i // tiles_per_group, 0, i % tiles_per_group))
    else:
        u_shape = (t, w_ssm)
        u_spec = pl.BlockSpec((tm, w_ssm), row)
        kv_shape = (t, w_att)
        kv_spec = pl.BlockSpec((tm, w_att), row)
    return pl.pallas_call(
        functools.partial(_inproj_kernel, w_att=w_att, scale=head_dim ** -0.5 * LOG2E,
                          kv_transposed=bool(time_major_batches)),
        grid=(t // tm,),
        in_specs=[pl.BlockSpec((tm, d), row),
                  pl.BlockSpec((1, mrows, d), grp),
                  pl.BlockSpec((1, mrows, d), grp),
                  _const_spec((1, d)),
                  _const_spec((d, n))],
        out_specs=[pl.BlockSpec((tm, w_att), row), kv_spec, kv_spec, pl.BlockSpec((tm, w_att), row),
                   pl.BlockSpec((tm, w_att), row), u_spec],
        out_shape=[jax.ShapeDtypeStruct((t, w_att), BF16),
                   jax.ShapeDtypeStruct(kv_shape, F32),
                   jax.ShapeDtypeStruct(kv_shape, F32),
                   jax.ShapeDtypeStruct((t, w_att), BF16),
                   jax.ShapeDtypeStruct((t, w_att), BF16),
                   jax.ShapeDtypeStruct(u_shape, F32)],
        compiler_params=_cparams("parallel"),
        name="in_proj",
    )(x2, sh, sc, g, w_bf)


def _sb_tile(qh, ks, vs, bias, carry, acc, umat, causal):
    z = _dot_nt(qh, ks) + bias
    sp = _softplus2(z)
    if causal is not None:
        sp = jnp.where(causal, sp, 0.0)
    suffix = _dot(sp.astype(BF16), umat)
    a = jnp.exp2(z - sp - suffix - carry)
    if causal is not None:
        a = jnp.where(causal, a, 0.0)
    acc = acc + _dot(a.astype(BF16), vs)
    carry = carry + suffix[:, 0:1] + sp[:, 0:1]
    return carry, acc


def _attn_kernel(bias_ref, q_ref, k_ref, v_ref, o_ref, *, tq, tk, head_dim):
    hp = pl.program_id(1)
    qi = pl.program_id(2)
    q = q_ref[0]
    heads_per_block = LANES // head_dim
    ratio = tq // tk
    lane_head = lax.broadcasted_iota(jnp.int32, (1, LANES), 1) // head_dim
    umat = _suffix_matrix(tk)
    t_idx = lax.broadcasted_iota(jnp.int32, (tq, tk), 0)
    s_idx = lax.broadcasted_iota(jnp.int32, (tq, tk), 1)
    owns = [lane_head == hh for hh in range(heads_per_block)]
    qhs = [jnp.where(own, q, jnp.zeros_like(q)) for own in owns]
    biases = [bias_ref[0, hp * heads_per_block + hh] for hh in range(heads_per_block)]

    def update(kt, state, causal, row0=0):
        start = pl.multiple_of(kt * tk, tk)
        ks = k_ref[0, pl.ds(start, tk), :]
        vs = v_ref[0, pl.ds(start, tk), :]
        new = []
        for hh in range(heads_per_block):
            carry, acc = state[hh]
            if row0:
                c2, a2 = _sb_tile(qhs[hh][row0:], ks, vs, biases[hh], carry[row0:], acc[row0:], umat, causal[row0:])
                new.append((jnp.concatenate([carry[:row0], c2], axis=0), jnp.concatenate([acc[:row0], a2], axis=0)))
            else:
                new.append(_sb_tile(qhs[hh], ks, vs, biases[hh], carry, acc, umat, causal))
        return tuple(new)

    state = tuple((jnp.zeros((tq, 1), F32), jnp.zeros((tq, LANES), F32)) for _ in range(heads_per_block))
    for r in reversed(range(ratio)):
        state = update(qi * ratio + r, state, (s_idx + r * tk) < t_idx, row0=r * tk)
    state = lax.fori_loop(0, qi * ratio, lambda j, st: update(qi * ratio - 1 - j, st, None), state)
    out = state[0][1]
    for hh in range(1, heads_per_block):
        out = jnp.where(owns[hh], state[hh][1], out)
    o_ref[0] = out.astype(o_ref.dtype)


def _prompt_attention(q, k, v, sb_bias, *, head_dim):
    b, l, w = q.shape
    tq = min(ATT_Q_TILE, l)
    tk = min(ATT_K_TILE, l)
    qspec = pl.BlockSpec((1, tq, LANES), lambda bi, hp, qi: (bi, qi, hp))
    kvspec = pl.BlockSpec((1, l, LANES), lambda bi, hp, qi: (bi, 0, hp))
    return pl.pallas_call(
        functools.partial(_attn_kernel, tq=tq, tk=tk, head_dim=head_dim),
        grid=(b, w // LANES, l // tq),
        in_specs=[pl.BlockSpec(memory_space=pltpu.SMEM), qspec, kvspec, kvspec],
        out_specs=qspec,
        out_shape=jax.ShapeDtypeStruct((b, l, w), BF16),
        compiler_params=_cparams("parallel", "parallel", "arbitrary"),
        name="attn_prompt",
    )(sb_bias.reshape(1, -1).astype(F32) * LOG2E, q, k, v)


def _sb_pages(qbd, kts, vts, bias, carry, acc, umat, valid):
    page = kts[0].shape[1]
    z = _dot(qbd, jnp.concatenate(kts, axis=1)) + bias
    sp = _softplus2(z)
    if valid is not None:
        sp = jnp.where(valid, sp, 0.0)
    probs = [None] * len(kts)
    for p in reversed(range(len(kts))):
        sl = slice(p * page, (p + 1) * page)
        sp_p = sp[:, sl]
        hi, lo = _split_bf16(sp_p)
        suffix = _dot(hi, umat) + _dot(lo, umat)
        probs[p] = jnp.exp2(z[:, sl] - sp_p - suffix - carry)
        carry = carry + jnp.sum(sp_p, axis=1, keepdims=True)
    a = jnp.concatenate(probs, axis=1)
    if valid is not None:
        a = jnp.where(valid, a, 0.0)
    acc = acc + _dot_nt(jnp.concatenate(vts, axis=1), a.astype(BF16))
    return carry, acc


def _sattn_kernel(pt_ref, q_ref, kn_ref, vn_ref, bias_ref, *refs, pages_per_step, q_rows, n_new):
    del pt_ref
    k_refs = refs[:pages_per_step]
    v_refs = refs[pages_per_step:2 * pages_per_step]
    o_ref, carry_ref, acc_ref = refs[2 * pages_per_step:]
    j = pl.program_id(1)
    qbd = q_ref[0]
    bias = bias_ref[...]
    hr = qbd.shape[0]
    page = kn_ref.shape[2]
    umat = _suffix_matrix(page)

    @pl.when(j == 0)
    def _():
        i_idx = lax.broadcasted_iota(jnp.int32, (hr, page), 0) % q_rows
        s_idx = lax.broadcasted_iota(jnp.int32, (hr, page), 1)
        valid = (s_idx < i_idx) & (s_idx < n_new)
        carry, acc = _sb_pages(qbd, [kn_ref[0]], [vn_ref[0]], bias, jnp.zeros(carry_ref.shape, F32),
                               jnp.zeros(acc_ref.shape, F32), umat, valid)
        carry_ref[...] = carry
        acc_ref[...] = acc

    def flat(ref):
        x = ref[0]
        return x.reshape(x.shape[0] * x.shape[1], x.shape[2]).astype(BF16)

    carry, acc = _sb_pages(qbd, [flat(r) for r in k_refs], [flat(r) for r in v_refs], bias,
                           carry_ref[...], acc_ref[...], umat, None)
    carry_ref[...] = carry
    acc_ref[...] = acc

    @pl.when(j == pl.num_programs(1) - 1)
    def _():
        o_ref[0] = acc


def _sample_attention(q, k_new, v_new, cache_k, cache_v, page_table, sb_bias, *, batch):
    n_pool, page, h, dh = cache_k.shape
    hd = h * dh
    n_new = q.shape[0] // batch
    n_pages = page_table.shape[1]
    pps = math.gcd(SAMPLE_PAGES_PER_STEP, n_pages)
    hr = h * SAMPLE_Q_ROWS
    q4 = jnp.pad(q.reshape(batch, n_new, h, dh), ((0, 0), (0, SAMPLE_Q_ROWS - n_new), (0, 0), (0, 0)))
    qbd = jnp.einsum("bihd,gh->bgihd", q4, jnp.eye(h, dtype=q.dtype)).reshape(batch, hr, hd)

    def new_t(a):
        a = a.reshape(batch, n_new, hd).transpose(0, 2, 1)
        return jnp.pad(a, ((0, 0), (0, 0), (0, page - n_new)))

    kc = cache_k.transpose(0, 2, 3, 1)
    vc = cache_v.transpose(0, 2, 3, 1)
    pt = page_table.reshape(-1).astype(jnp.int32)
    bias = jnp.repeat(sb_bias.astype(F32) * LOG2E, SAMPLE_Q_ROWS).reshape(hr, 1)

    def paged(p):
        return pl.BlockSpec((1, h, dh, page),
                            lambda bi, j, pt_ref: (pt_ref[bi * n_pages + n_pages - pps * (j + 1) + p], 0, 0, 0))

    new_b = pl.BlockSpec((1, hd, page), lambda bi, j, pt_ref: (bi, 0, 0))
    out = pl.pallas_call(
        functools.partial(_sattn_kernel, pages_per_step=pps, q_rows=SAMPLE_Q_ROWS, n_new=n_new),
        grid_spec=pltpu.PrefetchScalarGridSpec(
            num_scalar_prefetch=1,
            grid=(batch, n_pages // pps),
            in_specs=[pl.BlockSpec((1, hr, hd), lambda bi, j, pt_ref: (bi, 0, 0)), new_b, new_b,
                      pl.BlockSpec((hr, 1), lambda bi, j, pt_ref: (0, 0))]
                     + [paged(p) for p in range(pps)] * 2,
            out_specs=pl.BlockSpec((1, hd, hr), lambda bi, j, pt_ref: (bi, 0, 0)),
            scratch_shapes=[pltpu.VMEM((hr, 1), F32), pltpu.VMEM((hd, hr), F32)]),
        out_shape=jax.ShapeDtypeStruct((batch, hd, hr), F32),
        compiler_params=_cparams("parallel", "arbitrary"),
        name="attn_sample",
    )(pt, qbd, new_t(k_new), new_t(v_new), bias, *([kc] * pps), *([vc] * pps))
    o5 = out.reshape(batch, h, dh, h, SAMPLE_Q_ROWS)
    diag = jnp.einsum("bhdhi->bihd", o5)
    return diag[:, :n_new].reshape(batch * n_new, hd)


def _ssm_kernel(u_ref, h0r_ref, h0i_ref, ar_ref, ai_ref, bre_ref, bim_ref, brel_ref, biml_ref,
                cre_ref, cim_ref, d_ref, y_ref, hr_ref, hi_ref, xr_ref, xi_ref, sr_ref, si_ref, *,
                batch, steps, precise):
    step = pl.program_id(0)
    n_chunks = bre_ref.shape[0]
    cw = bre_ref.shape[1]
    sw = bre_ref.shape[2]
    n_state = n_chunks * sw

    @pl.when(step == 0)
    def _():
        sr_ref[...] = h0r_ref[...]
        si_ref[...] = h0i_ref[...]

    u = u_ref[...]
    u_hi, u_lo = _split_bf16(u)
    for c in range(n_chunks):
        uc = u_hi[:, c * cw:(c + 1) * cw]
        br = _dot(uc, bre_ref[c])
        bi = _dot(uc, bim_ref[c])
        if precise:
            ul = u_lo[:, c * cw:(c + 1) * cw]
            br = br + _dot(ul, bre_ref[c]) + _dot(uc, brel_ref[c])
            bi = bi + _dot(ul, bim_ref[c]) + _dot(uc, biml_ref[c])
        xr_ref[:, c * sw:(c + 1) * sw] = br
        xi_ref[:, c * sw:(c + 1) * sw] = bi

    lanes = min(SSM_CHUNK, n_state)
    for c in range(n_state // lanes):
        sl = pl.ds(c * lanes, lanes)
        a_r = jnp.broadcast_to(ar_ref[:, sl], (batch, lanes))
        a_i = jnp.broadcast_to(ai_ref[:, sl], (batch, lanes))

        def body(t, state):
            x_r, x_i = state
            rows = pl.ds(pl.multiple_of(t * batch, batch), batch)
            n_r = a_r * x_r - a_i * x_i + xr_ref[rows, sl]
            n_i = a_r * x_i + a_i * x_r + xi_ref[rows, sl]
            xr_ref[rows, sl] = n_r
            xi_ref[rows, sl] = n_i
            return n_r, n_i

        x_r, x_i = lax.fori_loop(0, steps, body, (sr_ref[:, sl], si_ref[:, sl]))
        sr_ref[:, sl] = x_r
        si_ref[:, sl] = x_i

    for c in range(n_chunks):
        x_r = xr_ref[:, c * sw:(c + 1) * sw].astype(BF16)
        x_i = xi_ref[:, c * sw:(c + 1) * sw].astype(BF16)
        ch = slice(c * cw, (c + 1) * cw)
        y_ref[:, ch] = _dot(x_r, cre_ref[c]) - _dot(x_i, cim_ref[c]) + d_ref[:, ch] * u[:, ch]

    @pl.when(step == pl.num_programs(0) - 1)
    def _():
        hr_ref[...] = sr_ref[...]
        hi_ref[...] = si_ref[...]


def _ssm_params(a_re, a_im, log_dt, b_re, b_im, c_re, c_im):
    g, n = a_re.shape
    hg = b_re.shape[2]
    dt = jnp.exp(log_dt.astype(F32))[:, None]
    mag = jnp.exp(dt * a_re)
    abar_re = mag * jnp.cos(dt * a_im)
    abar_im = mag * jnp.sin(dt * a_im)
    den = a_re * a_re + a_im * a_im
    nr = abar_re - 1.0
    ni = abar_im
    coef_re = (nr * a_re + ni * a_im) / den
    coef_im = (ni * a_re - nr * a_im) / den
    bbar_re = coef_re[..., None] * b_re - coef_im[..., None] * b_im
    bbar_im = coef_re[..., None] * b_im + coef_im[..., None] * b_re
    gpc = LANES // hg
    nc = g // gpc
    eye = jnp.eye(gpc, dtype=F32)

    def pack_b(bb):
        blk = bb.reshape(nc, gpc, n, hg)
        return jnp.einsum("kgnc,gh->kgchn", blk, eye).reshape(nc, gpc * hg, gpc * n)

    def pack_c(cc):
        blk = cc.reshape(nc, gpc, hg, n)
        return jnp.einsum("kgcn,gh->kgnhc", blk, eye).reshape(nc, gpc * n, gpc * hg)

    pb_re, pb_im = pack_b(bbar_re), pack_b(bbar_im)
    bre_hi, bre_lo = _split_bf16(pb_re)
    bim_hi, bim_lo = _split_bf16(pb_im)
    return dict(ar=abar_re.reshape(1, g * n), ai=abar_im.reshape(1, g * n),
                bre=bre_hi, bim=bim_hi, brel=bre_lo, biml=bim_lo,
                cre=pack_c(c_re.astype(F32)).astype(BF16), cim=pack_c(c_im.astype(F32)).astype(BF16))


def _ssm(u_tb, h0_re, h0_im, sp, d_skip, *, batch, steps, precise):
    rows, w = u_tb.shape
    n_state = sp["ar"].shape[1]
    blk = steps * batch
    full = lambda a: _const_spec(a.shape)
    args = (sp["ar"], sp["ai"], sp["bre"], sp["bim"], sp["brel"], sp["biml"], sp["cre"], sp["cim"],
            d_skip.reshape(1, w).astype(F32))
    return pl.pallas_call(
        functools.partial(_ssm_kernel, batch=batch, steps=steps, precise=precise),
        grid=(rows // blk,),
        in_specs=[pl.BlockSpec((blk, w), lambda i: (i, 0)), full(h0_re), full(h0_im)] + [full(a) for a in args],
        out_specs=[pl.BlockSpec((blk, w), lambda i: (i, 0)),
                   pl.BlockSpec((batch, n_state), lambda i: (0, 0)),
                   pl.BlockSpec((batch, n_state), lambda i: (0, 0))],
        out_shape=[jax.ShapeDtypeStruct((rows, w), F32),
                   jax.ShapeDtypeStruct((batch, n_state), F32),
                   jax.ShapeDtypeStruct((batch, n_state), F32)],
        scratch_shapes=[pltpu.VMEM((blk, n_state), F32), pltpu.VMEM((blk, n_state), F32),
                        pltpu.VMEM((batch, n_state), F32), pltpu.VMEM((batch, n_state), F32)],
        compiler_params=_cparams("arbitrary"),
        name="ssm",
    )(u_tb, h0_re, h0_im, *args)


def _post_kernel(*refs, n_experts, aliased):
    if aliased:
        refs = refs[1:]
    (x_ref, y_ref, o_ref, g1_ref, sh_ref, sc_ref, wglu_ref, bglu_ref, gs_ref, ga_ref, wout_ref, n2_ref,
     wrh_ref, wrl_ref, rb_ref, x1_ref, h2_ref, idx_ref, gate_ref) = refs
    y = jax.nn.gelu(y_ref[...], approximate=True)
    glu = jax.nn.sigmoid(_dot(y.astype(BF16), wglu_ref[...]) + bglu_ref[...])
    n_ssm = _rms(y * glu, gs_ref[...])
    n_att = _rms(o_ref[...].astype(F32), ga_ref[...])
    merged = jnp.concatenate([n_ssm, n_att], axis=-1).astype(BF16)
    x1 = x_ref[...] + g1_ref[0] * _dot(merged, wout_ref[...])
    x1_ref[...] = x1
    h2 = _rms(x1, n2_ref[...]) * (1.0 + sc_ref[0]) + sh_ref[0]
    h2_ref[...] = _pack_bf16_pairs(h2)

    h_hi, h_lo = _split_bf16(h2)
    logits = _dot(h_hi, wrh_ref[...]) + _dot(h_lo, wrh_ref[...]) + _dot(h_hi, wrl_ref[...])
    scores = jax.nn.sigmoid(logits)
    ranked = scores + rb_ref[...]
    tm = ranked.shape[0]
    e_lane = lax.broadcasted_iota(jnp.int32, (tm, n_experts), 1).astype(F32)
    o_lane = lax.broadcasted_iota(jnp.int32, (tm, IDX_LANES), 1)
    idx_out = jnp.zeros((tm, IDX_LANES), F32)
    sel_out = jnp.zeros((tm, IDX_LANES), F32)
    for k in range(TOP_K):
        best = jnp.max(ranked, axis=-1, keepdims=True)
        pick = jnp.min(jnp.where(ranked == best, e_lane, float(n_experts)), axis=-1, keepdims=True)
        chosen = e_lane == pick
        val = jnp.sum(jnp.where(chosen, scores, 0.0), axis=-1, keepdims=True)
        idx_out = jnp.where(o_lane == k, pick, idx_out)
        sel_out = jnp.where(o_lane == k, val, sel_out)
        ranked = jnp.where(chosen, -jnp.inf, ranked)
    idx_ref[...] = idx_out.astype(jnp.int32)
    gate_ref[...] = sel_out / jnp.sum(sel_out, axis=-1, keepdims=True) * ROUTED_SCALE


def _post(x2, y_src, y_spec, o_att, g1, sh2, sc2, pw, h2_prev, *, tm, tiles_per_group, h2_rows, row_offset):
    t, d = x2.shape
    w_att = o_att.shape[1]
    mrows = g1.shape[1]
    n_experts = pw["wr_hi"].shape[1]
    row = lambda i: (i, 0)
    grp = lambda i: (i // tiles_per_group, 0, 0)
    off = row_offset // tm
    weights = (pw["w_glu"], pw["b_glu"], pw["ssm_out_g"], pw["attn_out_g"], pw["w_out"], pw["norm2_g"],
               pw["wr_hi"], pw["wr_lo"], pw["router_bias"])
    in_specs = [pl.BlockSpec((tm, d), row), y_spec, pl.BlockSpec((tm, w_att), row),
                pl.BlockSpec((1, mrows, d), grp), pl.BlockSpec((1, mrows, d), grp), pl.BlockSpec((1, mrows, d), grp)]
    in_specs += [_const_spec(a.shape) for a in weights]
    args = (x2, y_src, o_att, g1, sh2, sc2) + weights
    aliases = {}
    aliased = h2_prev is not None
    if aliased:
        in_specs = [pl.BlockSpec(memory_space=pl.ANY)] + in_specs
        args = (h2_prev,) + args
        aliases = {0: 1}
    h2_shape = (h2_rows, d // 2)
    return pl.pallas_call(
        functools.partial(_post_kernel, n_experts=n_experts, aliased=aliased),
        grid=(t // tm,),
        in_specs=in_specs,
        out_specs=[pl.BlockSpec((tm, d), row),
                   pl.BlockSpec((tm, d // 2), lambda i: (i + off, 0)),
                   pl.BlockSpec((tm, IDX_LANES), row),
                   pl.BlockSpec((tm, IDX_LANES), row)],
        out_shape=[jax.ShapeDtypeStruct((t, d), F32),
                   jax.ShapeDtypeStruct(h2_shape, jnp.uint32),
                   jax.ShapeDtypeStruct((t, IDX_LANES), jnp.int32),
                   jax.ShapeDtypeStruct((t, IDX_LANES), F32)],
        input_output_aliases=aliases,
        compiler_params=_cparams("parallel"),
        name="post",
    )(*args)


def _row_copy(src_hbm, dst, sem, row, r):
    return pltpu.make_async_copy(src_hbm.at[pl.ds(row, 1), :], dst.at[pl.ds(r, 1), :], sem)


def _gather_rows(idx_ref, src_hbm, dst, sem, n):
    for r in range(n):
        _row_copy(src_hbm, dst, sem, idx_ref[0, 0, r], r).start()


def _wait_rows(src_hbm, dst, sem):
    pltpu.make_async_copy(src_hbm.at[pl.ds(0, dst.shape[0]), :], dst, sem).wait()


def _ring_steps(i, active, last, idx_refs, src_hbm, bufs, sem, n_rows, compute):
    @pl.when(i == 0)
    def _():
        for j in range(RING - 1):
            _gather_rows(idx_refs[j], src_hbm, bufs[j], sem.at[j], n_rows)

    def step(cur):
        ahead = (cur + RING - 1) % RING
        _wait_rows(src_hbm, bufs[cur], sem.at[cur])
        _gather_rows(idx_refs[RING - 1], src_hbm, bufs[ahead], sem.at[ahead], n_rows)
        compute(bufs[cur])

        @pl.when(last)
        def _():
            for j in range(1, RING):
                _wait_rows(src_hbm, bufs[(cur + j) % RING], sem.at[(cur + j) % RING])

    for cur in range(RING):
        pl.when(jnp.logical_and(active, i % RING == cur))(functools.partial(step, cur))


def _moe_kernel(be_ref, nused_ref, *refs, blk):
    idx_refs = refs[:RING]
    h_hbm, wg_ref, wu_ref, wd_ref, y_ref = refs[RING:RING + 5]
    bufs = refs[RING + 5:2 * RING + 5]
    sem, wg_bf, wu_bf, wd_bf = refs[2 * RING + 5:2 * RING + 9]
    i = pl.program_id(0)
    n_used = nused_ref[0]

    @pl.when(jnp.logical_and(i < n_used, jnp.logical_or(i == 0, be_ref[i] != be_ref[jnp.maximum(i - 1, 0)])))
    def _():
        wg_bf[...] = wg_ref[0].astype(BF16)
        wu_bf[...] = wu_ref[0].astype(BF16)
        wd_bf[...] = wd_ref[0].astype(BF16)

    def compute(xbuf):
        x = _unpack_bf16_pairs(xbuf[...])
        gate = _dot(x, wg_bf[...])
        up = _dot(x, wu_bf[...])
        act = (gate * jax.nn.sigmoid(gate) * up).astype(BF16)
        y_ref[...] = _dot(act, wd_bf[...])

    _ring_steps(i, i < n_used, i == n_used - 1, idx_refs, h_hbm, bufs, sem, blk, compute)

    @pl.when(i >= n_used)
    def _():
        y_ref[...] = jnp.zeros(y_ref.shape, y_ref.dtype)


def _moe(h_all, rows_tok, block_expert, n_used, wg, wu, wd):
    n_blocks = block_expert.shape[0]
    blk = MOE_BLOCK
    e, d, f = wg.shape
    idx3 = rows_tok.reshape(n_blocks, 1, blk)
    smem_blk = lambda im: pl.BlockSpec((1, 1, blk), im, memory_space=pltpu.SMEM)
    return pl.pallas_call(
        functools.partial(_moe_kernel, blk=blk),
        grid_spec=pltpu.PrefetchScalarGridSpec(
            num_scalar_prefetch=2,
            grid=(n_blocks,),
            in_specs=[smem_blk(functools.partial(lambda j, i, be, nu: (jnp.minimum(i + j, n_blocks - 1), 0, 0), j))
                      for j in range(RING)]
                     + [pl.BlockSpec(memory_space=pl.ANY),
                        pl.BlockSpec((1, d, f), lambda i, be, nu: (be[i], 0, 0)),
                        pl.BlockSpec((1, d, f), lambda i, be, nu: (be[i], 0, 0)),
                        pl.BlockSpec((1, f, d), lambda i, be, nu: (be[i], 0, 0))],
            out_specs=pl.BlockSpec((blk, d), lambda i, be, nu: (i, 0)),
            scratch_shapes=[pltpu.VMEM((blk, d // 2), jnp.uint32)] * RING
                           + [pltpu.SemaphoreType.DMA((RING,)), pltpu.VMEM((d, f), BF16), pltpu.VMEM((d, f), BF16),
                              pltpu.VMEM((f, d), BF16)]),
        out_shape=jax.ShapeDtypeStruct((n_blocks * blk, d), F32),
        compiler_params=_cparams("arbitrary"),
        name="moe",
    )(block_expert, n_used, *([idx3] * RING), h_all, wg, wu, wd)


def _final_kernel(*refs):
    pos_refs = refs[:RING]
    y_hbm, x1_ref, h2_ref, gate_ref, g2_ref, wsg_ref, wsu_ref, wsd_ref, nf_ref, o_ref = refs[RING:RING + 10]
    bufs = refs[RING + 10:2 * RING + 10]
    sem = refs[2 * RING + 10]
    i = pl.program_id(0)
    n = pl.num_programs(0)
    tm = x1_ref.shape[0]

    def compute(ybuf):
        gates = gate_ref[...]
        routed = gates[:, 0:1] * ybuf[pl.ds(0, tm), :]
        for k in range(1, TOP_K):
            routed = routed + gates[:, k:k + 1] * ybuf[pl.ds(k * tm, tm), :]
        h = _unpack_bf16_pairs(h2_ref[...])
        gate = _dot(h, wsg_ref[...])
        up = _dot(h, wsu_ref[...])
        shared = _dot((gate * jax.nn.sigmoid(gate) * up).astype(BF16), wsd_ref[...])
        x = x1_ref[...] + g2_ref[0] * (routed + shared)
        o_ref[...] = _rms(x, nf_ref[...])

    _ring_steps(i, i >= 0, i == n - 1, pos_refs, y_hbm, bufs, sem, TOP_K * tm, compute)


def _final(y_rows, pos, x1, h2_all, gates, g2, fw, *, tm, tiles_per_group, row_offset):
    t, d = x1.shape
    n_tiles = t // tm
    mrows = g2.shape[1]
    pos3 = pos.reshape(n_tiles, tm, TOP_K).transpose(0, 2, 1).reshape(n_tiles, 1, TOP_K * tm)
    smem_blk = lambda im: pl.BlockSpec((1, 1, TOP_K * tm), im, memory_space=pltpu.SMEM)
    row = lambda i: (i, 0)
    off = row_offset // tm
    weights = (fw["ws_gate"], fw["ws_up"], fw["ws_down"], fw["normf_g"])
    return pl.pallas_call(
        _final_kernel,
        grid=(n_tiles,),
        in_specs=[smem_blk(functools.partial(lambda j, i: (jnp.minimum(i + j, n_tiles - 1), 0, 0), j))
                  for j in range(RING)]
                 + [pl.BlockSpec(memory_space=pl.ANY),
                  pl.BlockSpec((tm, d), row),
                  pl.BlockSpec((tm, d // 2), lambda i: (i + off, 0)),
                  pl.BlockSpec((tm, IDX_LANES), row),
                  pl.BlockSpec((1, mrows, d), lambda i: (i // tiles_per_group, 0, 0))]
                 + [_const_spec(a.shape) for a in weights],
        out_specs=pl.BlockSpec((tm, d), row),
        out_shape=jax.ShapeDtypeStruct((t, d), F32),
        scratch_shapes=[pltpu.VMEM((TOP_K * tm, d), F32)] * RING + [pltpu.SemaphoreType.DMA((RING,))],
        compiler_params=_cparams("arbitrary"),
        name="final",
    )(*([pos3] * RING), y_rows, x1, h2_all, gates, g2, *weights)


def _dispatch(idx, n_experts):
    t, k = idx.shape
    blk = MOE_BLOCK
    n_assign = t * k
    n_blocks = -(-n_assign // blk) + n_experts
    onehot = jnp.sum((idx[:, :, None] == jnp.arange(n_experts, dtype=jnp.int32)).astype(jnp.int32), axis=1)
    before = jnp.cumsum(onehot, axis=0) - onehot
    counts = jnp.sum(onehot, axis=0)
    padded = (counts + blk - 1) // blk * blk
    pad_end = jnp.cumsum(padded)
    pad_start = pad_end - padded
    pos = pad_start[idx] + jnp.take_along_axis(before, idx, axis=1)
    tok = jnp.broadcast_to(jnp.arange(t, dtype=jnp.int32)[:, None], (t, k))
    rows_tok = jnp.zeros((n_blocks * blk,), jnp.int32).at[pos.reshape(-1)].set(tok.reshape(-1))
    block_start = jnp.arange(n_blocks, dtype=jnp.int32) * blk
    owner = jnp.sum((pad_end[None, :] <= block_start[:, None]).astype(jnp.int32), axis=1)
    block_expert = jnp.minimum(owner, n_experts - 1).astype(jnp.int32)
    n_used = (pad_end[-1] // blk).astype(jnp.int32).reshape(1)
    return pos.astype(jnp.int32), rows_tok, block_expert, n_used


def kernel(x_prompt, x_sample, c_prompt, c_sample, cache_k, cache_v, state_ssm_re, state_ssm_im, page_table,
           norm1_g, norm2_g, w_ada, b_ada, w_in, sb_bias, ssm_a_re, ssm_a_im, ssm_log_dt, ssm_b_re, ssm_b_im,
           ssm_c_re, ssm_c_im, ssm_d, w_glu, b_glu, ssm_out_g, attn_out_g, w_out, w_router, router_bias,
           we_gate, we_up, we_down, ws_gate, ws_up, ws_down, normf_g):
    depth = w_in.shape[0]
    assert depth == 1, "single-layer trunk"
    bp, lp, d = x_prompt.shape
    bs, ls, _ = x_sample.shape
    n_heads, head_dim = cache_k.shape[3], cache_k.shape[4]
    w_att = n_heads * head_dim
    g_ssm, n_ssm = ssm_a_re.shape[1], ssm_a_re.shape[2]
    n_experts = w_router.shape[2]
    tp, ts = bp * lp, bs * ls
    t_all = tp + ts
    tm = min(256, lp)
    assert lp % tm == 0 and ts % 8 == 0 and tp % FINAL_TILE == 0 and ts % FINAL_TILE == 0

    c_all = jnp.concatenate([c_prompt, c_sample], axis=0)
    mod = _modulation(c_all, w_ada[0], b_ada[0]).reshape(bp + bs, N_MOD, d)
    mod_p = [mod[:bp, i][:, None, :] for i in range(N_MOD)]
    mod_s = [jnp.repeat(mod[bp:, i], ls, axis=0)[None] for i in range(N_MOD)]

    w_in_bf = w_in[0].astype(BF16)
    g1n = norm1_g[0].reshape(1, d)
    sp = _ssm_params(ssm_a_re[0].astype(F32), ssm_a_im[0].astype(F32), ssm_log_dt[0], ssm_b_re[0].astype(F32),
                     ssm_b_im[0].astype(F32), ssm_c_re[0], ssm_c_im[0])
    wr_hi, wr_lo = _split_bf16(w_router[0].astype(F32))
    pw = dict(w_glu=w_glu[0].astype(BF16), b_glu=b_glu[0].reshape(1, -1), ssm_out_g=ssm_out_g[0].reshape(1, -1),
              attn_out_g=attn_out_g[0].reshape(1, -1), w_out=w_out[0].astype(BF16), norm2_g=norm2_g[0].reshape(1, d),
              wr_hi=wr_hi, wr_lo=wr_lo, router_bias=router_bias[0].reshape(1, -1).astype(F32))
    fw = dict(ws_gate=ws_gate[0].astype(BF16), ws_up=ws_up[0].astype(BF16), ws_down=ws_down[0].astype(BF16),
              normf_g=normf_g.reshape(1, d))

    xp2 = x_prompt.reshape(tp, d)
    tiles_pb = lp // tm
    q_p, k_p, v_p, kb_p, vb_p, u_p = _in_proj(xp2, mod_p[0], mod_p[1], g1n, w_in_bf, w_att=w_att, head_dim=head_dim,
                                              tm=tm, tiles_per_group=tiles_pb, time_major_batches=bp)
    o_att_p = _prompt_attention(q_p.reshape(bp, lp, w_att), kb_p.reshape(bp, lp, w_att),
                                vb_p.reshape(bp, lp, w_att), sb_bias[0], head_dim=head_dim).reshape(tp, w_att)
    w_ssm = u_p.shape[1] // bp
    zeros_state = jnp.zeros((bp, g_ssm * n_ssm), F32)
    ssm_steps = max(1, min(lp, 256 // bp))
    y_p, hr_p, hi_p = _ssm(u_p.reshape(lp * bp, w_ssm), zeros_state, zeros_state, sp, ssm_d[0],
                           batch=bp, steps=ssm_steps, precise=False)
    y_p_spec = pl.BlockSpec((tm, w_ssm), lambda i: (i % tiles_pb, i // tiles_pb))
    x1_p, h2_all, idx_p, gate_p = _post(xp2, y_p.reshape(lp, bp * w_ssm), y_p_spec, o_att_p, mod_p[2], mod_p[3],
                                        mod_p[4], pw, None, tm=tm, tiles_per_group=tiles_pb, h2_rows=t_all,
                                        row_offset=0)

    xs2 = x_sample.reshape(ts, d)
    q_s, k_s, v_s, kb_s, vb_s, u_s = _in_proj(xs2, mod_s[0], mod_s[1], g1n, w_in_bf, w_att=w_att, head_dim=head_dim,
                                              tm=ts, tiles_per_group=1, time_major_batches=0)
    assert ls <= SAMPLE_Q_ROWS
    o_att_s = _sample_attention(q_s, kb_s, vb_s, cache_k[0], cache_v[0], page_table, sb_bias[0],
                                batch=bs).astype(BF16)
    u_s_tb = u_s.reshape(bs, ls, w_ssm).transpose(1, 0, 2).reshape(ls * bs, w_ssm)
    y_s_tb, hr_s, hi_s = _ssm(u_s_tb, state_ssm_re[0].reshape(bs, -1).astype(F32),
                              state_ssm_im[0].reshape(bs, -1).astype(F32), sp, ssm_d[0],
                              batch=bs, steps=ls, precise=True)
    y_s = y_s_tb.reshape(ls, bs, w_ssm).transpose(1, 0, 2).reshape(ts, w_ssm)
    x1_s, h2_all, idx_s, gate_s = _post(xs2, y_s, pl.BlockSpec((ts, w_ssm), lambda i: (i, 0)), o_att_s, mod_s[2],
                                        mod_s[3], mod_s[4], pw, h2_all, tm=ts, tiles_per_group=1, h2_rows=t_all,
                                        row_offset=tp)

    idx_all = jnp.concatenate([idx_p[:, :TOP_K], idx_s[:, :TOP_K]], axis=0)
    pos, rows_tok, block_expert, n_used = _dispatch(idx_all, n_experts)
    y_rows = _moe(h2_all, rows_tok, block_expert, n_used, we_gate[0], we_up[0], we_down[0])
    ft = FINAL_TILE
    out_p = _final(y_rows, pos[:tp], x1_p, h2_all, gate_p, mod_p[5], fw, tm=ft, tiles_per_group=lp // ft,
                   row_offset=0)
    out_s = _final(y_rows, pos[tp:], x1_s, h2_all, gate_s, mod_s[5], fw, tm=ft, tiles_per_group=ts // ft,
                   row_offset=tp)

    kv_p = (1, bp, lp, n_heads, head_dim)
    kv_s = (1, bs, ls, n_heads, head_dim)
    st_p = (1, bp, g_ssm, n_ssm)
    st_s = (1, bs, g_ssm, n_ssm)
    def kv_prompt(a):
        return a.reshape(bp, n_heads, head_dim, lp).transpose(0, 3, 1, 2).reshape(kv_p)

    return (out_p.reshape(bp, lp, d), out_s.reshape(bs, ls, d),
            kv_prompt(k_p), kv_prompt(v_p), hr_p.reshape(st_p), hi_p.reshape(st_p),
            k_s.reshape(kv_s), v_s.reshape(kv_s), hr_s.reshape(st_s), hi_s.reshape(st_s))
```

```python
import functools
import math

import jax
import jax.numpy as jnp
from jax import lax
from jax.experimental import pallas as pl
from jax.experimental.pallas import tpu as pltpu

F32 = jnp.float32
BF16 = jnp.bfloat16

RMS_EPS = 1e-6
TOP_K = 6
ROUTED_SCALE = 2.5
N_MOD = 6

LANES = 128
VMEM_LIMIT = 56 << 20
MOE_BLOCK = 256
FINAL_TILE = 128
RING = 3
ATT_Q_TILE = 2048
ATT_K_TILE = 256
SSM_CHUNK = 512
IDX_LANES = 128
SAMPLE_PAGES_PER_STEP = 8
SAMPLE_Q_ROWS = 8


def _cparams(*sem):
    return pltpu.CompilerParams(dimension_semantics=sem, vmem_limit_bytes=VMEM_LIMIT)


def _const_spec(shape):
    nd = len(shape)
    return pl.BlockSpec(shape, lambda *_: (0,) * nd, pipeline_mode=pl.Buffered(1))


def _rms(x, g):
    inv = lax.rsqrt(jnp.mean(x * x, axis=-1, keepdims=True) + RMS_EPS)
    return x * inv * g


LOG2E = math.log2(math.e)


def _softplus2(z2):
    return jnp.maximum(z2, 0.0) + jnp.log2(1.0 + jnp.exp2(-jnp.abs(z2)))


def _split_bf16(x):
    hi = x.astype(BF16)
    lo = (x - hi.astype(F32)).astype(BF16)
    return hi, lo


def _dot(a, b):
    return jnp.dot(a, b, preferred_element_type=F32)


def _dot_nt(a, b):
    return lax.dot_general(a, b, (((1,), (1,)), ((), ())), preferred_element_type=F32)


def _pack_bf16_pairs(x):
    half = x.shape[1] // 2
    lo = lax.bitcast_convert_type(x[:, :half].astype(BF16).astype(F32), jnp.uint32) >> 16
    hi = lax.bitcast_convert_type(x[:, half:].astype(BF16).astype(F32), jnp.uint32) & jnp.uint32(0xFFFF0000)
    return lo | hi


def _unpack_bf16_pairs(p):
    lo = lax.bitcast_convert_type(p << 16, F32).astype(BF16)
    hi = lax.bitcast_convert_type(p & jnp.uint32(0xFFFF0000), F32).astype(BF16)
    return jnp.concatenate([lo, hi], axis=1)


def _suffix_matrix(n):
    j = lax.broadcasted_iota(jnp.int32, (n, n), 0)
    s = lax.broadcasted_iota(jnp.int32, (n, n), 1)
    return (j > s).astype(BF16)


def _mod_kernel(c_ref, w_ref, b_ref, o_ref):
    c = c_ref[...]
    a = (c * jax.nn.sigmoid(c)).astype(BF16)
    o_ref[...] = _dot(a, w_ref[...].astype(BF16)) + b_ref[...]


def _modulation(c, w_ada, b_ada):
    rows, d = c.shape
    n = w_ada.shape[1]
    tn = 1024
    return pl.pallas_call(
        _mod_kernel,
        grid=(n // tn,),
        in_specs=[_const_spec((rows, d)),
                  pl.BlockSpec((d, tn), lambda j: (0, j)),
                  pl.BlockSpec((1, tn), lambda j: (0, j))],
        out_specs=pl.BlockSpec((rows, tn), lambda j: (0, j)),
        out_shape=jax.ShapeDtypeStruct((rows, n), F32),
        compiler_params=_cparams("parallel"),
        name="mod",
    )(c, w_ada, b_ada.reshape(1, n))


def _inproj_kernel(x_ref, sh_ref, sc_ref, g_ref, w_ref, q_ref, k_ref, v_ref, kb_ref, vb_ref, u_ref, *, w_att, scale,
                   kv_transposed):
    x = x_ref[...]
    h = (_rms(x, g_ref[...]) * (1.0 + sc_ref[0]) + sh_ref[0]).astype(BF16)
    q = _dot(h, w_ref[:, 0:w_att])
    q_ref[...] = (q * scale).astype(BF16)
    k = _dot(h, w_ref[:, w_att:2 * w_att])
    kb_ref[...] = k.astype(BF16)
    v = _dot(h, w_ref[:, 2 * w_att:3 * w_att])
    vb_ref[...] = v.astype(BF16)
    if kv_transposed:
        k_ref[0] = k.T
        v_ref[0] = v.T
    else:
        k_ref[...] = k
        v_ref[...] = v
    u_ref[...] = _dot(h, w_ref[:, 3 * w_att:])


def _in_proj(x2, sh, sc, g, w_bf, *, w_att, head_dim, tm, tiles_per_group, time_major_batches):
    t, d = x2.shape
    n = w_bf.shape[1]
    w_ssm = n - 3 * w_att
    mrows = sh.shape[1]
    row = lambda i: (i, 0)
    grp = lambda i: (i // tiles_per_group, 0, 0)
    if time_major_batches:
        seq = t // time_major_batches
        u_shape = (seq, time_major_batches * w_ssm)
        u_spec = pl.BlockSpec((tm, w_ssm), lambda i: (i % tiles_per_group, i // tiles_per_group))
        kv_shape = (time_major_batches, w_att, seq)
        kv_spec = pl.BlockSpec((1, w_att, tm), lambda i: (i // tiles_per_group, 0, i % tiles_per_group))
    else:
        u_shape = (t, w_ssm)
        u_spec = pl.BlockSpec((tm, w_ssm), row)
        kv_shape = (t, w_att)
        kv_spec = pl.BlockSpec((tm, w_att), row)
    return pl.pallas_call(
        functools.partial(_inproj_kernel, w_att=w_att, scale=head_dim ** -0.5 * LOG2E,
                          kv_transposed=bool(time_major_batches)),
        grid=(t // tm,),
        in_specs=[pl.BlockSpec((tm, d), row),
                  pl.BlockSpec((1, mrows, d), grp),
                  pl.BlockSpec((1, mrows, d), grp),
                  _const_spec((1, d)),
                  _const_spec((d, n))],
        out_specs=[pl.BlockSpec((tm, w_att), row), kv_spec, kv_spec, pl.BlockSpec((tm, w_att), row),
                   pl.BlockSpec((tm, w_att), row), u_spec],
        out_shape=[jax.ShapeDtypeStruct((t, w_att), BF16),
                   jax.ShapeDtypeStruct(kv_shape, F32),
                   jax.ShapeDtypeStruct(kv_shape, F32),
                   jax.ShapeDtypeStruct((t, w_att), BF16),
                   jax.ShapeDtypeStruct((t, w_att), BF16),
                   jax.ShapeDtypeStruct(u_shape, F32)],
        compiler_params=_cparams("parallel"),
        name="in_proj",
    )(x2, sh, sc, g, w_bf)


def _sb_tile(qh, ks, vs, bias, carry, acc, umat, causal):
    z = _dot_nt(qh, ks) + bias
    sp = _softplus2(z)
    if causal is not None:
        sp = jnp.where(causal, sp, 0.0)
    suffix = _dot(sp.astype(BF16), umat)
    a = jnp.exp2(z - sp - suffix - carry)
    if causal is not None:
        a = jnp.where(causal, a, 0.0)
    acc = acc + _dot(a.astype(BF16), vs)
    carry = carry + suffix[:, 0:1] + sp[:, 0:1]
    return carry, acc


def _attn_kernel(bias_ref, q_ref, k_ref, v_ref, o_ref, *, tq, tk, head_dim):
    hp = pl.program_id(1)
    qi = pl.program_id(2)
    q = q_ref[0]
    heads_per_block = LANES // head_dim
    ratio = tq // tk
    lane_head = lax.broadcasted_iota(jnp.int32, (1, LANES), 1) // head_dim
    umat = _suffix_matrix(tk)
    t_idx = lax.broadcasted_iota(jnp.int32, (tq, tk), 0)
    s_idx = lax.broadcasted_iota(jnp.int32, (tq, tk), 1)
    owns = [lane_head == hh for hh in range(heads_per_block)]
    qhs = [jnp.where(own, q, jnp.zeros_like(q)) for own in owns]
    biases = [bias_ref[0, hp * heads_per_block + hh] for hh in range(heads_per_block)]

    def update(kt, state, causal, row0=0):
        start = pl.multiple_of(kt * tk, tk)
        ks = k_ref[0, pl.ds(start, tk), :]
        vs = v_ref[0, pl.ds(start, tk), :]
        new = []
        for hh in range(heads_per_block):
            carry, acc = state[hh]
            if row0:
                c2, a2 = _sb_tile(qhs[hh][row0:], ks, vs, biases[hh], carry[row0:], acc[row0:], umat, causal[row0:])
                new.append((jnp.concatenate([carry[:row0], c2], axis=0), jnp.concatenate([acc[:row0], a2], axis=0)))
            else:
                new.append(_sb_tile(qhs[hh], ks, vs, biases[hh], carry, acc, umat, causal))
        return tuple(new)

    state = tuple((jnp.zeros((tq, 1), F32), jnp.zeros((tq, LANES), F32)) for _ in range(heads_per_block))
    for r in reversed(range(ratio)):
        state = update(qi * ratio + r, state, (s_idx + r * tk) < t_idx, row0=r * tk)
    state = lax.fori_loop(0, qi * ratio, lambda j, st: update(qi * ratio - 1 - j, st, None), state)
    out = state[0][1]
    for hh in range(1, heads_per_block):
        out = jnp.where(owns[hh], state[hh][1], out)
    o_ref[0] = out.astype(o_ref.dtype)


def _prompt_attention(q, k, v, sb_bias, *, head_dim):
    b, l, w = q.shape
    tq = min(ATT_Q_TILE, l)
    tk = min(ATT_K_TILE, l)
    qspec = pl.BlockSpec((1, tq, LANES), lambda bi, hp, qi: (bi, qi, hp))
    kvspec = pl.BlockSpec((1, l, LANES), lambda bi, hp, qi: (bi, 0, hp))
    return pl.pallas_call(
        functools.partial(_attn_kernel, tq=tq, tk=tk, head_dim=head_dim),
        grid=(b, w // LANES, l // tq),
        in_specs=[pl.BlockSpec(memory_space=pltpu.SMEM), qspec, kvspec, kvspec],
        out_specs=qspec,
        out_shape=jax.ShapeDtypeStruct((b, l, w), BF16),
        compiler_params=_cparams("parallel", "parallel", "arbitrary"),
        name="attn_prompt",
    )(sb_bias.reshape(1, -1).astype(F32) * LOG2E, q, k, v)


def _sb_pages(qbd, kts, vts, bias, carry, acc, umat, valid):
    page = kts[0].shape[1]
    z = _dot(qbd, jnp.concatenate(kts, axis=1)) + bias
    sp = _softplus2(z)
    if valid is not None:
        sp = jnp.where(valid, sp, 0.0)
    probs = [None] * len(kts)
    for p in reversed(range(len(kts))):
        sl = slice(p * page, (p + 1) * page)
        sp_p = sp[:, sl]
        hi, lo = _split_bf16(sp_p)
        suffix = _dot(hi, umat) + _dot(lo, umat)
        probs[p] = jnp.exp2(z[:, sl] - sp_p - suffix - carry)
        carry = carry + jnp.sum(sp_p, axis=1, keepdims=True)
    a = jnp.concatenate(probs, axis=1)
    if valid is not None:
        a = jnp.where(valid, a, 0.0)
    acc = acc + _dot_nt(jnp.concatenate(vts, axis=1), a.astype(BF16))
    return carry, acc


def _sattn_kernel(pt_ref, q_ref, kn_ref, vn_ref, bias_ref, *refs, pages_per_step, q_rows, n_new):
    del pt_ref
    k_refs = refs[:pages_per_step]
    v_refs = refs[pages_per_step:2 * pages_per_step]
    o_ref, carry_ref, acc_ref = refs[2 * pages_per_step:]
    j = pl.program_id(1)
    qbd = q_ref[0]
    bias = bias_ref[...]
    hr = qbd.shape[0]
    page = kn_ref.shape[2]
    umat = _suffix_matrix(page)

    @pl.when(j == 0)
    def _():
        i_idx = lax.broadcasted_iota(jnp.int32, (hr, page), 0) % q_rows
        s_idx = lax.broadcasted_iota(jnp.int32, (hr, page), 1)
        valid = (s_idx < i_idx) & (s_idx < n_new)
        carry, acc = _sb_pages(qbd, [kn_ref[0]], [vn_ref[0]], bias, jnp.zeros(carry_ref.shape, F32),
                               jnp.zeros(acc_ref.shape, F32), umat, valid)
        carry_ref[...] = carry
        acc_ref[...] = acc

    def flat(ref):
        x = ref[0]
        return x.reshape(x.shape[0] * x.shape[1], x.shape[2]).astype(BF16)

    carry, acc = _sb_pages(qbd, [flat(r) for r in k_refs], [flat(r) for r in v_refs], bias,
                           carry_ref[...], acc_ref[...], umat, None)
    carry_ref[...] = carry
    acc_ref[...] = acc

    @pl.when(j == pl.num_programs(1) - 1)
    def _():
        o_ref[0] = acc


def _sample_attention(q, k_new, v_new, cache_k, cache_v, page_table, sb_bias, *, batch):
    n_pool, page, h, dh = cache_k.shape
    hd = h * dh
    n_new = q.shape[0] // batch
    n_pages = page_table.shape[1]
    pps = math.gcd(SAMPLE_PAGES_PER_STEP, n_pages)
    hr = h * SAMPLE_Q_ROWS
    q4 = jnp.pad(q.reshape(batch, n_new, h, dh), ((0, 0), (0, SAMPLE_Q_ROWS - n_new), (0, 0), (0, 0)))
    qbd = jnp.einsum("bihd,gh->bgihd", q4, jnp.eye(h, dtype=q.dtype)).reshape(batch, hr, hd)

    def new_t(a):
        a = a.reshape(batch, n_new, hd).transpose(0, 2, 1)
        return jnp.pad(a, ((0, 0), (0, 0), (0, page - n_new)))

    kc = cache_k.transpose(0, 2, 3, 1)
    vc = cache_v.transpose(0, 2, 3, 1)
    pt = page_table.reshape(-1).astype(jnp.int32)
    bias = jnp.repeat(sb_bias.astype(F32) * LOG2E, SAMPLE_Q_ROWS).reshape(hr, 1)

    def paged(p):
        return pl.BlockSpec((1, h, dh, page),
                            lambda bi, j, pt_ref: (pt_ref[bi * n_pages + n_pages - pps * (j + 1) + p], 0, 0, 0))

    new_b = pl.BlockSpec((1, hd, page), lambda bi, j, pt_ref: (bi, 0, 0))
    out = pl.pallas_call(
        functools.partial(_sattn_kernel, pages_per_step=pps, q_rows=SAMPLE_Q_ROWS, n_new=n_new),
        grid_spec=pltpu.PrefetchScalarGridSpec(
            num_scalar_prefetch=1,
            grid=(batch, n_pages // pps),
            in_specs=[pl.BlockSpec((1, hr, hd), lambda bi, j, pt_ref: (bi, 0, 0)), new_b, new_b,
                      pl.BlockSpec((hr, 1), lambda bi, j, pt_ref: (0, 0))]
                     + [paged(p) for p in range(pps)] * 2,
            out_specs=pl.BlockSpec((1, hd, hr), lambda bi, j, pt_ref: (bi, 0, 0)),
            scratch_shapes=[pltpu.VMEM((hr, 1), F32), pltpu.VMEM((hd, hr), F32)]),
        out_shape=jax.ShapeDtypeStruct((batch, hd, hr), F32),
        compiler_params=_cparams("parallel", "arbitrary"),
        name="attn_sample",
    )(pt, qbd, new_t(k_new), new_t(v_new), bias, *([kc] * pps), *([vc] * pps))
    o5 = out.reshape(batch, h, dh, h, SAMPLE_Q_ROWS)
    diag = jnp.einsum("bhdhi->bihd", o5)
    return diag[:, :n_new].reshape(batch * n_new, hd)


def _ssm_kernel(u_ref, h0r_ref, h0i_ref, ar_ref, ai_ref, bre_ref, bim_ref, brel_ref, biml_ref,
                cre_ref, cim_ref, d_ref, y_ref, hr_ref, hi_ref, xr_ref, xi_ref, sr_ref, si_ref, *,
                batch, steps, precise):
    step = pl.program_id(0)
    n_chunks = bre_ref.shape[0]
    cw = bre_ref.shape[1]
    sw = bre_ref.shape[2]
    n_state = n_chunks * sw

    @pl.when(step == 0)
    def _():
        sr_ref[...] = h0r_ref[...]
        si_ref[...] = h0i_ref[...]

    u = u_ref[...]
    u_hi, u_lo = _split_bf16(u)
    for c in range(n_chunks):
        uc = u_hi[:, c * cw:(c + 1) * cw]
        br = _dot(uc, bre_ref[c])
        bi = _dot(uc, bim_ref[c])
        if precise:
            ul = u_lo[:, c * cw:(c + 1) * cw]
            br = br + _dot(ul, bre_ref[c]) + _dot(uc, brel_ref[c])
            bi = bi + _dot(ul, bim_ref[c]) + _dot(uc, biml_ref[c])
        xr_ref[:, c * sw:(c + 1) * sw] = br
        xi_ref[:, c * sw:(c + 1) * sw] = bi

    lanes = min(SSM_CHUNK, n_state)
    for c in range(n_state // lanes):
        sl = pl.ds(c * lanes, lanes)
        a_r = jnp.broadcast_to(ar_ref[:, sl], (batch, lanes))
        a_i = jnp.broadcast_to(ai_ref[:, sl], (batch, lanes))

        def body(t, state):
            x_r, x_i = state
            rows = pl.ds(pl.multiple_of(t * batch, batch), batch)
            n_r = a_r * x_r - a_i * x_i + xr_ref[rows, sl]
            n_i = a_r * x_i + a_i * x_r + xi_ref[rows, sl]
            xr_ref[rows, sl] = n_r
            xi_ref[rows, sl] = n_i
            return n_r, n_i

        x_r, x_i = lax.fori_loop(0, steps, body, (sr_ref[:, sl], si_ref[:, sl]))
        sr_ref[:, sl] = x_r
        si_ref[:, sl] = x_i

    for c in range(n_chunks):
        x_r = xr_ref[:, c * sw:(c + 1) * sw].astype(BF16)
        x_i = xi_ref[:, c * sw:(c + 1) * sw].astype(BF16)
        ch = slice(c * cw, (c + 1) * cw)
        y_ref[:, ch] = _dot(x_r, cre_ref[c]) - _dot(x_i, cim_ref[c]) + d_ref[:, ch] * u[:, ch]

    @pl.when(step == pl.num_programs(0) - 1)
    def _():
        hr_ref[...] = sr_ref[...]
        hi_ref[...] = si_ref[...]


def _ssm_params(a_re, a_im, log_dt, b_re, b_im, c_re, c_im):
    g, n = a_re.shape
    hg = b_re.shape[2]
    dt = jnp.exp(log_dt.astype(F32))[:, None]
    mag = jnp.exp(dt * a_re)
    abar_re = mag * jnp.cos(dt * a_im)
    abar_im = mag * jnp.sin(dt * a_im)
    den = a_re * a_re + a_im * a_im
    nr = abar_re - 1.0
    ni = abar_im
    coef_re = (nr * a_re + ni * a_im) / den
    coef_im = (ni * a_re - nr * a_im) / den
    bbar_re = coef_re[..., None] * b_re - coef_im[..., None] * b_im
    bbar_im = coef_re[..., None] * b_im + coef_im[..., None] * b_re
    gpc = LANES // hg
    nc = g // gpc
    eye = jnp.eye(gpc, dtype=F32)

    def pack_b(bb):
        blk = bb.reshape(nc, gpc, n, hg)
        return jnp.einsum("kgnc,gh->kgchn", blk, eye).reshape(nc, gpc * hg, gpc * n)

    def pack_c(cc):
        blk = cc.reshape(nc, gpc, hg, n)
        return jnp.einsum("kgcn,gh->kgnhc", blk, eye).reshape(nc, gpc * n, gpc * hg)

    pb_re, pb_im = pack_b(bbar_re), pack_b(bbar_im)
    bre_hi, bre_lo = _split_bf16(pb_re)
    bim_hi, bim_lo = _split_bf16(pb_im)
    return dict(ar=abar_re.reshape(1, g * n), ai=abar_im.reshape(1, g * n),
                bre=bre_hi, bim=bim_hi, brel=bre_lo, biml=bim_lo,
                cre=pack_c(c_re.astype(F32)).astype(BF16), cim=pack_c(c_im.astype(F32)).astype(BF16))


def _ssm(u_tb, h0_re, h0_im, sp, d_skip, *, batch, steps, precise):
    rows, w = u_tb.shape
    n_state = sp["ar"].shape[1]
    blk = steps * batch
    full = lambda a: _const_spec(a.shape)
    args = (sp["ar"], sp["ai"], sp["bre"], sp["bim"], sp["brel"], sp["biml"], sp["cre"], sp["cim"],
            d_skip.reshape(1, w).astype(F32))
    return pl.pallas_call(
        functools.partial(_ssm_kernel, batch=batch, steps=steps, precise=precise),
        grid=(rows // blk,),
        in_specs=[pl.BlockSpec((blk, w), lambda i: (i, 0)), full(h0_re), full(h0_im)] + [full(a) for a in args],
        out_specs=[pl.BlockSpec((blk, w), lambda i: (i, 0)),
                   pl.BlockSpec((batch, n_state), lambda i: (0, 0)),
                   pl.BlockSpec((batch, n_state), lambda i: (0, 0))],
        out_shape=[jax.ShapeDtypeStruct((rows, w), F32),
                   jax.ShapeDtypeStruct((batch, n_state), F32),
                   jax.ShapeDtypeStruct((batch, n_state), F32)],
        scratch_shapes=[pltpu.VMEM((blk, n_state), F32), pltpu.VMEM((blk, n_state), F32),
                        pltpu.VMEM((batch, n_state), F32), pltpu.VMEM((batch, n_state), F32)],
        compiler_params=_cparams("arbitrary"),
        name="ssm",
    )(u_tb, h0_re, h0_im, *args)


def _post_kernel(*refs, n_experts, aliased):
    if aliased:
        refs = refs[1:]
    (x_ref, y_ref, o_ref, g1_ref, sh_ref, sc_ref, wglu_ref, bglu_ref, gs_ref, ga_ref, wout_ref, n2_ref,
     wrh_ref, wrl_ref, rb_ref, x1_ref, h2_ref, idx_ref, gate_ref) = refs
    y = jax.nn.gelu(y_ref[...], approximate=True)
    glu = jax.nn.sigmoid(_dot(y.astype(BF16), wglu_ref[...]) + bglu_ref[...])
    n_ssm = _rms(y * glu, gs_ref[...])
    n_att = _rms(o_ref[...].astype(F32), ga_ref[...])
    merged = jnp.concatenate([n_ssm, n_att], axis=-1).astype(BF16)
    x1 = x_ref[...] + g1_ref[0] * _dot(merged, wout_ref[...])
    x1_ref[...] = x1
    h2 = _rms(x1, n2_ref[...]) * (1.0 + sc_ref[0]) + sh_ref[0]
    h2_ref[...] = _pack_bf16_pairs(h2)

    h_hi, h_lo = _split_bf16(h2)
    logits = _dot(h_hi, wrh_ref[...]) + _dot(h_lo, wrh_ref[...]) + _dot(h_hi, wrl_ref[...])
    scores = jax.nn.sigmoid(logits)
    ranked = scores + rb_ref[...]
    tm = ranked.shape[0]
    e_lane = lax.broadcasted_iota(jnp.int32, (tm, n_experts), 1).astype(F32)
    o_lane = lax.broadcasted_iota(jnp.int32, (tm, IDX_LANES), 1)
    idx_out = jnp.zeros((tm, IDX_LANES), F32)
    sel_out = jnp.zeros((tm, IDX_LANES), F32)
    for k in range(TOP_K):
        best = jnp.max(ranked, axis=-1, keepdims=True)
        pick = jnp.min(jnp.where(ranked == best, e_lane, float(n_experts)), axis=-1, keepdims=True)
        chosen = e_lane == pick
        val = jnp.sum(jnp.where(chosen, scores, 0.0), axis=-1, keepdims=True)
        idx_out = jnp.where(o_lane == k, pick, idx_out)
        sel_out = jnp.where(o_lane == k, val, sel_out)
        ranked = jnp.where(chosen, -jnp.inf, ranked)
    idx_ref[...] = idx_out.astype(jnp.int32)
    gate_ref[...] = sel_out / jnp.sum(sel_out, axis=-1, keepdims=True) * ROUTED_SCALE


def _post(x2, y_src, y_spec, o_att, g1, sh2, sc2, pw, h2_prev, *, tm, tiles_per_group, h2_rows, row_offset):
    t, d = x2.shape
    w_att = o_att.shape[1]
    mrows = g1.shape[1]
    n_experts = pw["wr_hi"].shape[1]
    row = lambda i: (i, 0)
    grp = lambda i: (i // tiles_per_group, 0, 0)
    off = row_offset // tm
    weights = (pw["w_glu"], pw["b_glu"], pw["ssm_out_g"], pw["attn_out_g"], pw["w_out"], pw["norm2_g"],
               pw["wr_hi"], pw["wr_lo"], pw["router_bias"])
    in_specs = [pl.BlockSpec((tm, d), row), y_spec, pl.BlockSpec((tm, w_att), row),
                pl.BlockSpec((1, mrows, d), grp), pl.BlockSpec((1, mrows, d), grp), pl.BlockSpec((1, mrows, d), grp)]
    in_specs += [_const_spec(a.shape) for a in weights]
    args = (x2, y_src, o_att, g1, sh2, sc2) + weights
    aliases = {}
    aliased = h2_prev is not None
    if aliased:
        in_specs = [pl.BlockSpec(memory_space=pl.ANY)] + in_specs
        args = (h2_prev,) + args
        aliases = {0: 1}
    h2_shape = (h2_rows, d // 2)
    return pl.pallas_call(
        functools.partial(_post_kernel, n_experts=n_experts, aliased=aliased),
        grid=(t // tm,),
        in_specs=in_specs,
        out_specs=[pl.BlockSpec((tm, d), row),
                   pl.BlockSpec((tm, d // 2), lambda i: (i + off, 0)),
                   pl.BlockSpec((tm, IDX_LANES), row),
                   pl.BlockSpec((tm, IDX_LANES), row)],
        out_shape=[jax.ShapeDtypeStruct((t, d), F32),
                   jax.ShapeDtypeStruct(h2_shape, jnp.uint32),
                   jax.ShapeDtypeStruct((t, IDX_LANES), jnp.int32),
                   jax.ShapeDtypeStruct((t, IDX_LANES), F32)],
        input_output_aliases=aliases,
        compiler_params=_cparams("parallel"),
        name="post",
    )(*args)


def _row_copy(src_hbm, dst, sem, row, r):
    return pltpu.make_async_copy(src_hbm.at[pl.ds(row, 1), :], dst.at[pl.ds(r, 1), :], sem)


def _gather_rows(idx_ref, src_hbm, dst, sem, n):
    for r in range(n):
        _row_copy(src_hbm, dst, sem, idx_ref[0, 0, r], r).start(priority=r % 2)


def _wait_rows(src_hbm, dst, sem):
    pltpu.make_async_copy(src_hbm.at[pl.ds(0, dst.shape[0]), :], dst, sem).wait()


def _ring_steps(i, active, last, idx_refs, src_hbm, bufs, sem, n_rows, compute):
    @pl.when(i == 0)
    def _():
        for j in range(RING - 1):
            _gather_rows(idx_refs[j], src_hbm, bufs[j], sem.at[j], n_rows)

    def step(cur):
        ahead = (cur + RING - 1) % RING
        _wait_rows(src_hbm, bufs[cur], sem.at[cur])
        _gather_rows(idx_refs[RING - 1], src_hbm, bufs[ahead], sem.at[ahead], n_rows)
        compute(bufs[cur])

        @pl.when(last)
        def _():
            for j in range(1, RING):
                _wait_rows(src_hbm, bufs[(cur + j) % RING], sem.at[(cur + j) % RING])

    for cur in range(RING):
        pl.when(jnp.logical_and(active, i % RING == cur))(functools.partial(step, cur))


def _moe_kernel(be_ref, nused_ref, *refs, blk):
    idx_refs = refs[:RING]
    h_hbm, wg_ref, wu_ref, wd_ref, y_ref = refs[RING:RING + 5]
    bufs = refs[RING + 5:2 * RING + 5]
    sem, wg_bf, wu_bf, wd_bf = refs[2 * RING + 5:2 * RING + 9]
    i = pl.program_id(0)
    n_used = nused_ref[0]

    @pl.when(jnp.logical_and(i < n_used, jnp.logical_or(i == 0, be_ref[i] != be_ref[jnp.maximum(i - 1, 0)])))
    def _():
        wg_bf[...] = wg_ref[0].astype(BF16)
        wu_bf[...] = wu_ref[0].astype(BF16)
        wd_bf[...] = wd_ref[0].astype(BF16)

    def compute(xbuf):
        x = _unpack_bf16_pairs(xbuf[...])
        gate = _dot(x, wg_bf[...])
        up = _dot(x, wu_bf[...])
        act = (gate * jax.nn.sigmoid(gate) * up).astype(BF16)
        y_ref[...] = _dot(act, wd_bf[...])

    _ring_steps(i, i < n_used, i == n_used - 1, idx_refs, h_hbm, bufs, sem, blk, compute)

    @pl.when(i >= n_used)
    def _():
        y_ref[...] = jnp.zeros(y_ref.shape, y_ref.dtype)


def _moe(h_all, rows_tok, block_expert, n_used, wg, wu, wd):
    n_blocks = block_expert.shape[0]
    blk = MOE_BLOCK
    e, d, f = wg.shape
    idx3 = rows_tok.reshape(n_blocks, 1, blk)
    smem_blk = lambda im: pl.BlockSpec((1, 1, blk), im, memory_space=pltpu.SMEM)
    return pl.pallas_call(
        functools.partial(_moe_kernel, blk=blk),
        grid_spec=pltpu.PrefetchScalarGridSpec(
            num_scalar_prefetch=2,
            grid=(n_blocks,),
            in_specs=[smem_blk(functools.partial(lambda j, i, be, nu: (jnp.minimum(i + j, n_blocks - 1), 0, 0), j))
                      for j in range(RING)]
                     + [pl.BlockSpec(memory_space=pl.ANY),
                        pl.BlockSpec((1, d, f), lambda i, be, nu: (be[i], 0, 0)),
                        pl.BlockSpec((1, d, f), lambda i, be, nu: (be[i], 0, 0)),
                        pl.BlockSpec((1, f, d), lambda i, be, nu: (be[i], 0, 0))],
            out_specs=pl.BlockSpec((blk, d), lambda i, be, nu: (i, 0)),
            scratch_shapes=[pltpu.VMEM((blk, d // 2), jnp.uint32)] * RING
                           + [pltpu.SemaphoreType.DMA((RING,)), pltpu.VMEM((d, f), BF16), pltpu.VMEM((d, f), BF16),
                              pltpu.VMEM((f, d), BF16)]),
        out_shape=jax.ShapeDtypeStruct((n_blocks * blk, d), F32),
        compiler_params=_cparams("arbitrary"),
        name="moe",
    )(block_expert, n_used, *([idx3] * RING), h_all, wg, wu, wd)


def _final_kernel(*refs):
    pos_refs = refs[:RING]
    y_hbm, x1_ref, h2_ref, gate_ref, g2_ref, wsg_ref, wsu_ref, wsd_ref, nf_ref, o_ref = refs[RING:RING + 10]
    bufs = refs[RING + 10:2 * RING + 10]
    sem = refs[2 * RING + 10]
    i = pl.program_id(0)
    n = pl.num_programs(0)
    tm = x1_ref.shape[0]

    def compute(ybuf):
        gates = gate_ref[...]
        routed = gates[:, 0:1] * ybuf[pl.ds(0, tm), :]
        for k in range(1, TOP_K):
            routed = routed + gates[:, k:k + 1] * ybuf[pl.ds(k * tm, tm), :]
        h = _unpack_bf16_pairs(h2_ref[...])
        gate = _dot(h, wsg_ref[...])
        up = _dot(h, wsu_ref[...])
        shared = _dot((gate * jax.nn.sigmoid(gate) * up).astype(BF16), wsd_ref[...])
        x = x1_ref[...] + g2_ref[0] * (routed + shared)
        o_ref[...] = _rms(x, nf_ref[...])

    _ring_steps(i, i >= 0, i == n - 1, pos_refs, y_hbm, bufs, sem, TOP_K * tm, compute)


def _final(y_rows, pos, x1, h2_all, gates, g2, fw, *, tm, tiles_per_group, row_offset):
    t, d = x1.shape
    n_tiles = t // tm
    mrows = g2.shape[1]
    pos3 = pos.reshape(n_tiles, tm, TOP_K).transpose(0, 2, 1).reshape(n_tiles, 1, TOP_K * tm)
    smem_blk = lambda im: pl.BlockSpec((1, 1, TOP_K * tm), im, memory_space=pltpu.SMEM)
    row = lambda i: (i, 0)
    off = row_offset // tm
    weights = (fw["ws_gate"], fw["ws_up"], fw["ws_down"], fw["normf_g"])
    return pl.pallas_call(
        _final_kernel,
        grid=(n_tiles,),
        in_specs=[smem_blk(functools.partial(lambda j, i: (jnp.minimum(i + j, n_tiles - 1), 0, 0), j))
                  for j in range(RING)]
                 + [pl.BlockSpec(memory_space=pl.ANY),
                  pl.BlockSpec((tm, d), row),
                  pl.BlockSpec((tm, d // 2), lambda i: (i + off, 0)),
                  pl.BlockSpec((tm, IDX_LANES), row),
                  pl.BlockSpec((1, mrows, d), lambda i: (i // tiles_per_group, 0, 0))]
                 + [_const_spec(a.shape) for a in weights],
        out_specs=pl.BlockSpec((tm, d), row),
        out_shape=jax.ShapeDtypeStruct((t, d), F32),
        scratch_shapes=[pltpu.VMEM((TOP_K * tm, d), F32)] * RING + [pltpu.SemaphoreType.DMA((RING,))],
        compiler_params=_cparams("arbitrary"),
        name="final",
    )(*([pos3] * RING), y_rows, x1, h2_all, gates, g2, *weights)


def _dispatch(idx, n_experts):
    t, k = idx.shape
    blk = MOE_BLOCK
    n_assign = t * k
    n_blocks = -(-n_assign // blk) + n_experts
    onehot = jnp.sum((idx[:, :, None] == jnp.arange(n_experts, dtype=jnp.int32)).astype(jnp.int32), axis=1)
    before = jnp.cumsum(onehot, axis=0) - onehot
    counts = jnp.sum(onehot, axis=0)
    padded = (counts + blk - 1) // blk * blk
    pad_end = jnp.cumsum(padded)
    pad_start = pad_end - padded
    pos = pad_start[idx] + jnp.take_along_axis(before, idx, axis=1)
    tok = jnp.broadcast_to(jnp.arange(t, dtype=jnp.int32)[:, None], (t, k))
    rows_tok = jnp.zeros((n_blocks * blk,), jnp.int32).at[pos.reshape(-1)].set(tok.reshape(-1))
    block_start = jnp.arange(n_blocks, dtype=jnp.int32) * blk
    owner = jnp.sum((pad_end[None, :] <= block_start[:, None]).astype(jnp.int32), axis=1)
    block_expert = jnp.minimum(owner, n_experts - 1).astype(jnp.int32)
    n_used = (pad_end[-1] // blk).astype(jnp.int32).reshape(1)
    return pos.astype(jnp.int32), rows_tok, block_expert, n_used


def kernel(x_prompt, x_sample, c_prompt, c_sample, cache_k, cache_v, state_ssm_re, state_ssm_im, page_table,
           norm1_g, norm2_g, w_ada, b_ada, w_in, sb_bias, ssm_a_re, ssm_a_im, ssm_log_dt, ssm_b_re, ssm_b_im,
           ssm_c_re, ssm_c_im, ssm_d, w_glu, b_glu, ssm_out_g, attn_out_g, w_out, w_router, router_bias,
           we_gate, we_up, we_down, ws_gate, ws_up, ws_down, normf_g):
    depth = w_in.shape[0]
    assert depth == 1, "single-layer trunk"
    bp, lp, d = x_prompt.shape
    bs, ls, _ = x_sample.shape
    n_heads, head_dim = cache_k.shape[3], cache_k.shape[4]
    w_att = n_heads * head_dim
    g_ssm, n_ssm = ssm_a_re.shape[1], ssm_a_re.shape[2]
    n_experts = w_router.shape[2]
    tp, ts = bp * lp, bs * ls
    t_all = tp + ts
    tm = min(256, lp)
    assert lp % tm == 0 and ts % 8 == 0 and tp % FINAL_TILE == 0 and ts % FINAL_TILE == 0

    c_all = jnp.concatenate([c_prompt, c_sample], axis=0)
    mod = _modulation(c_all, w_ada[0], b_ada[0]).reshape(bp + bs, N_MOD, d)
    mod_p = [mod[:bp, i][:, None, :] for i in range(N_MOD)]
    mod_s = [jnp.repeat(mod[bp:, i], ls, axis=0)[None] for i in range(N_MOD)]

    w_in_bf = w_in[0].astype(BF16)
    g1n = norm1_g[0].reshape(1, d)
    sp = _ssm_params(ssm_a_re[0].astype(F32), ssm_a_im[0].astype(F32), ssm_log_dt[0], ssm_b_re[0].astype(F32),
                     ssm_b_im[0].astype(F32), ssm_c_re[0], ssm_c_im[0])
    wr_hi, wr_lo = _split_bf16(w_router[0].astype(F32))
    pw = dict(w_glu=w_glu[0].astype(BF16), b_glu=b_glu[0].reshape(1, -1), ssm_out_g=ssm_out_g[0].reshape(1, -1),
              attn_out_g=attn_out_g[0].reshape(1, -1), w_out=w_out[0].astype(BF16), norm2_g=norm2_g[0].reshape(1, d),
              wr_hi=wr_hi, wr_lo=wr_lo, router_bias=router_bias[0].reshape(1, -1).astype(F32))
    fw = dict(ws_gate=ws_gate[0].astype(BF16), ws_up=ws_up[0].astype(BF16), ws_down=ws_down[0].astype(BF16),
              normf_g=normf_g.reshape(1, d))

    xp2 = x_prompt.reshape(tp, d)
    tiles_pb = lp // tm
    q_p, k_p, v_p, kb_p, vb_p, u_p = _in_proj(xp2, mod_p[0], mod_p[1], g1n, w_in_bf, w_att=w_att, head_dim=head_dim,
                                              tm=tm, tiles_per_group=tiles_pb, time_major_batches=bp)
    o_att_p = _prompt_attention(q_p.reshape(bp, lp, w_att), kb_p.reshape(bp, lp, w_att),
                                vb_p.reshape(bp, lp, w_att), sb_bias[0], head_dim=head_dim).reshape(tp, w_att)
    w_ssm = u_p.shape[1] // bp
    zeros_state = jnp.zeros((bp, g_ssm * n_ssm), F32)
    ssm_steps = max(1, min(lp, 256 // bp))
    y_p, hr_p, hi_p = _ssm(u_p.reshape(lp * bp, w_ssm), zeros_state, zeros_state, sp, ssm_d[0],
                           batch=bp, steps=ssm_steps, precise=False)
    y_p_spec = pl.BlockSpec((tm, w_ssm), lambda i: (i % tiles_pb, i // tiles_pb))
    x1_p, h2_all, idx_p, gate_p = _post(xp2, y_p.reshape(lp, bp * w_ssm), y_p_spec, o_att_p, mod_p[2], mod_p[3],
                                        mod_p[4], pw, None, tm=tm, tiles_per_group=tiles_pb, h2_rows=t_all,
                                        row_offset=0)

    xs2 = x_sample.reshape(ts, d)
    q_s, k_s, v_s, kb_s, vb_s, u_s = _in_proj(xs2, mod_s[0], mod_s[1], g1n, w_in_bf, w_att=w_att, head_dim=head_dim,
                                              tm=ts, tiles_per_group=1, time_major_batches=0)
    assert ls <= SAMPLE_Q_ROWS
    o_att_s = _sample_attention(q_s, kb_s, vb_s, cache_k[0], cache_v[0], page_table, sb_bias[0],
                                batch=bs).astype(BF16)
    u_s_tb = u_s.reshape(bs, ls, w_ssm).transpose(1, 0, 2).reshape(ls * bs, w_ssm)
    y_s_tb, hr_s, hi_s = _ssm(u_s_tb, state_ssm_re[0].reshape(bs, -1).astype(F32),
                              state_ssm_im[0].reshape(bs, -1).astype(F32), sp, ssm_d[0],
                              batch=bs, steps=ls, precise=True)
    y_s = y_s_tb.reshape(ls, bs, w_ssm).transpose(1, 0, 2).reshape(ts, w_ssm)
    x1_s, h2_all, idx_s, gate_s = _post(xs2, y_s, pl.BlockSpec((ts, w_ssm), lambda i: (i, 0)), o_att_s, mod_s[2],
                                        mod_s[3], mod_s[4], pw, h2_all, tm=ts, tiles_per_group=1, h2_rows=t_all,
                                        row_offset=tp)

    idx_all = jnp.concatenate([idx_p[:, :TOP_K], idx_s[:, :TOP_K]], axis=0)
    pos, rows_tok, block_expert, n_used = _dispatch(idx_all, n_experts)
    y_rows = _moe(h2_all, rows_tok, block_expert, n_used, we_gate[0], we_up[0], we_down[0])
    ft = FINAL_TILE
    out_p = _final(y_rows, pos[:tp], x1_p, h2_all, gate_p, mod_p[5], fw, tm=ft, tiles_per_group=lp // ft,
                   row_offset=0)
    out_s = _final(y_rows, pos[tp:], x1_s, h2_all, gate_s, mod_s[5], fw, tm=ft, tiles_per_group=ts // ft,
                   row_offset=tp)

    kv_p = (1, bp, lp, n_heads, head_dim)
    kv_s = (1, bs, ls, n_heads, head_dim)
    st_p = (1, bp, g_ssm, n_ssm)
    st_s = (1, bs, g_ssm, n_ssm)
    def kv_prompt(a):
        return a.reshape(bp, n_heads, head_dim, lp).transpose(0, 3, 1, 2).reshape(kv_p)

    return (out_p.reshape(bp, lp, d), out_s.reshape(bs, ls, d),
            kv_prompt(k_p), kv_prompt(v_p), hr_p.reshape(st_p), hi_p.reshape(st_p),
            k_s.reshape(kv_s), v_s.reshape(kv_s), hr_s.reshape(st_s), hi_s.reshape(st_s))
```
